```python
import math
import jax, jax.numpy as jnp
from jax import lax
import numpy as np

D_MODEL = 1024
BATCH = 4
SEQ = 8192
DEPTH = 2

N_A_LAYERS = DEPTH // 2
N_B_LAYERS = DEPTH - N_A_LAYERS

N_HEADS = 8
Q_BLOCK = 128

DIFF_HEAD_DIM = 64
DIFF_V_DIM = 2 * DIFF_HEAD_DIM
DIFF_QK_WIDTH = N_HEADS * 2 * DIFF_HEAD_DIM
DIFF_V_WIDTH = N_HEADS * DIFF_V_DIM

NSA_HEAD_DIM = 128
NSA_KV_GROUPS = 2
NSA_HEADS_PER_GROUP = N_HEADS // NSA_KV_GROUPS
N_BRANCH = 3
CMP_LEN = 32
CMP_STRIDE = 16
CMP_HIDDEN = 256
SEL_BLOCK = 64
SEL_TOPN = 16
SEL_N_LOCAL = 2
WINDOW = 512
NSA_Q_BLOCK = 64
NSA_Q_WIDTH = N_HEADS * NSA_HEAD_DIM
NSA_GATE_WIDTH = N_HEADS * N_BRANCH
NSA_KV_WIDTH = N_BRANCH * 2 * NSA_KV_GROUPS * NSA_HEAD_DIM

REL_BUCKETS = 32
REL_MAX_DIST = 1024

PEER_HEADS = 8
PEER_N_KEYS = 128
PEER_N_EXPERTS = PEER_N_KEYS ** 2
PEER_QUERY_DIM = 256
PEER_TOPK = 16
PEER_TOKEN_BLOCK = 128

N_ADA = 6
NEG_INF = -1e30
FORCE = 1e9
EPS = 1e-6

kernel_name = 'yoco_diffattn_nsa_peer_hybrid'


def rms_norm(x, gain):
    xf = x.astype(jnp.float32)
    y = xf * lax.rsqrt(jnp.mean(xf * xf, axis=-1, keepdims=True) + EPS)
    return (y * gain.astype(jnp.float32)).astype(x.dtype)


def modulate(h, shift, scale):
    return h * (1 + scale[:, None, :]) + shift[:, None, :]


def rel_bucket(dist):
    max_exact = REL_BUCKETS // 2
    n = jnp.maximum(dist, 0)
    nf = jnp.maximum(n, 1).astype(jnp.float32)
    large = max_exact + (jnp.log(nf / max_exact) / math.log(REL_MAX_DIST / max_exact)
                         * (REL_BUCKETS - max_exact)).astype(jnp.int32)
    large = jnp.minimum(large, REL_BUCKETS - 1)
    return jnp.where(n < max_exact, n, large)


def masked_softmax(logits, mask):
    lf = jnp.where(mask, logits.astype(jnp.float32), NEG_INF)
    p = jax.nn.softmax(lf, axis=-1)
    return jnp.where(mask, p, 0.0)


def diff_attention(h, w_in, w_out, lam, q_gain, k_gain, subln_gain, rel_bias, layer_idx):
    B, S, _ = h.shape
    proj = h @ w_in
    q, k, v = jnp.split(proj, [DIFF_QK_WIDTH, 2 * DIFF_QK_WIDTH], axis=-1)
    q = rms_norm(q.reshape(B, S, N_HEADS, 2, DIFF_HEAD_DIM), q_gain)
    k = rms_norm(k.reshape(B, S, N_HEADS, 2, DIFF_HEAD_DIM), k_gain)
    v = v.reshape(B, S, N_HEADS, DIFF_V_DIM)
    lam_init = 0.8 - 0.6 * math.exp(-0.3 * layer_idx)
    lamf = lam.astype(jnp.float32)
    lam_full = jnp.exp(jnp.sum(lamf[0] * lamf[1])) - jnp.exp(jnp.sum(lamf[2] * lamf[3])) + lam_init
    nb = S // Q_BLOCK
    qb = q.reshape(B, nb, Q_BLOCK, N_HEADS, 2, DIFF_HEAD_DIM).transpose(1, 0, 2, 3, 4, 5)
    k_pos = jnp.arange(S)
    scale = DIFF_HEAD_DIM ** -0.5

    def block(args):
        qi, i = args
        q_pos = i * Q_BLOCK + jnp.arange(Q_BLOCK)
        dist = q_pos[:, None] - k_pos[None, :]
        mask = dist >= 0
        bias = rel_bias[rel_bucket(dist)].transpose(2, 0, 1)[None, :, None]
        logits = jnp.einsum('bthjd,bshjd->bhjts', qi, k) * scale + bias
        p = masked_softmax(logits, mask)
        attn = p[:, :, 0] - lam_full * p[:, :, 1]
        return jnp.einsum('bhts,bshe->bthe', attn.astype(v.dtype), v)

    o = lax.map(block, (qb, jnp.arange(nb)))
    o = o.transpose(1, 0, 2, 3, 4).reshape(B, S, N_HEADS, DIFF_V_DIM)
    o = rms_norm(o, subln_gain) * (1 - lam_init)
    return o.reshape(B, S, DIFF_V_WIDTH) @ w_out


def nsa_shared_kv(x, c_act, kv_norm, kv_ada_w, kv_ada_b, kv_w, cmp_pos, cmp_w1, cmp_b1, cmp_w2, k_gain):
    B, S, _ = x.shape
    G, HD = NSA_KV_GROUPS, NSA_HEAD_DIM
    shift, scale = jnp.split(c_act @ kv_ada_w + kv_ada_b, 2, axis=-1)
    h = modulate(rms_norm(x, kv_norm), shift, scale)
    kv = (h @ kv_w).reshape(B, S, N_BRANCH, 2, G, HD)
    kv = kv.transpose(2, 3, 0, 4, 1, 5)
    n_cmp = (S - CMP_LEN) // CMP_STRIDE + 1
    idx = np.arange(n_cmp)[:, None] * CMP_STRIDE + np.arange(CMP_LEN)[None, :]
    blocks = kv[0][:, :, :, idx]
    blocks = blocks + cmp_pos[:, None, None, None]
    flat = blocks.reshape(2, B, G, n_cmp, CMP_LEN * HD)
    hid = jax.nn.gelu(jnp.einsum('kbgnf,kfh->kbgnh', flat, cmp_w1) + cmp_b1[:, None, None, None])
    cmp = jnp.einsum('kbgnh,khd->kbgnd', hid, cmp_w2)
    k_cmp = rms_norm(cmp[0], k_gain[0])
    v_cmp = cmp[1]
    k_slc = rms_norm(kv[1, 0], k_gain[1])
    v_slc = kv[1, 1]
    k_win = rms_norm(kv[2, 0], k_gain[2])
    v_win = kv[2, 1]
    return (k_cmp, v_cmp, k_slc, v_slc, k_win, v_win)


def nsa_attention(h, w_in, b_gate, w_out, q_gain, rel_bias, k_cmp, v_cmp, k_slc, v_slc, k_win, v_win):
    B, S, _ = h.shape
    G, HG, HD, T = NSA_KV_GROUPS, NSA_HEADS_PER_GROUP, NSA_HEAD_DIM, NSA_Q_BLOCK
    proj = h @ w_in
    q = rms_norm(proj[..., :NSA_Q_WIDTH].reshape(B, S, G, HG, HD), q_gain)
    gates = jax.nn.sigmoid((proj[..., NSA_Q_WIDTH:] + b_gate).astype(jnp.float32))
    nb = S // T
    n_cmp = k_cmp.shape[2]
    n_sel = S // SEL_BLOCK
    top_n = min(SEL_TOPN, n_sel)
    cmp_start = np.arange(n_cmp) * CMP_STRIDE
    sel_start = np.arange(n_sel) * SEL_BLOCK
    overlap = np.clip(np.minimum(cmp_start[:, None] + CMP_LEN, sel_start[None, :] + SEL_BLOCK)
                      - np.maximum(cmp_start[:, None], sel_start[None, :]), 0, None) / CMP_LEN
    overlap = jnp.asarray(overlap, jnp.float32)
    cmp_end = jnp.asarray(cmp_start + CMP_LEN - 1, jnp.int32)
    k_blocks = k_slc.reshape(B, G, n_sel, SEL_BLOCK, HD)
    v_blocks = v_slc.reshape(B, G, n_sel, SEL_BLOCK, HD)
    pad = ((0, 0), (0, 0), (WINDOW, 0), (0, 0))
    k_win_p = jnp.pad(k_win, pad)
    v_win_p = jnp.pad(v_win, pad)
    tab = rel_bias.reshape(REL_BUCKETS, G, HG)
    qb = q.reshape(B, nb, T, G, HG, HD).transpose(1, 0, 3, 4, 2, 5)
    gb = gates.reshape(B, nb, T, G, HG, N_BRANCH).transpose(1, 0, 3, 4, 2, 5)
    scale = HD ** -0.5
    b_idx = jnp.arange(B)[:, None, None, None]
    g_idx = jnp.arange(G)[None, :, None, None]
    g_idx5 = jnp.arange(G)[None, :, None, None, None]
    blk = jnp.arange(n_sel)

    def block(args):
        qi, gi, i = args
        q_pos = i * T + jnp.arange(T)
        dist_c = q_pos[:, None] - cmp_end[None, :]
        mask_c = dist_c >= 0
        bias_c = tab[rel_bucket(dist_c)].transpose(2, 3, 0, 1)
        logit_c = jnp.einsum('bghtd,bgnd->bghtn', qi, k_cmp) * scale + bias_c
        p_c = masked_softmax(logit_c, mask_c)
        o_c = jnp.einsum('bghtn,bgnd->bghtd', p_c.astype(v_cmp.dtype), v_cmp)
        imp = jnp.einsum('bghtn,nj->bgtj', p_c, overlap)
        cur = q_pos // SEL_BLOCK
        back = cur[:, None] - blk[None, :]
        forced = (blk[None, :] == 0) | ((back >= 0) & (back < SEL_N_LOCAL))
        imp = jnp.where(forced, FORCE, imp)
        imp = jnp.where(back >= 0, imp, NEG_INF)
        top_score, sel = lax.top_k(imp, top_n)
        sel_valid = top_score >= 0
        ks = k_blocks[b_idx, g_idx, sel]
        vs = v_blocks[b_idx, g_idx, sel]
        key_pos = sel[..., None] * SEL_BLOCK + jnp.arange(SEL_BLOCK)
        dist_s = q_pos[:, None, None] - key_pos
        mask_s = ((dist_s >= 0) & sel_valid[..., None]).reshape(B, G, 1, T, top_n * SEL_BLOCK)
        bias_s = tab[rel_bucket(dist_s), g_idx5].transpose(0, 1, 5, 2, 3, 4)
        logit_s = jnp.einsum('bghtd,bgtnrd->bghtnr', qi, ks) * scale + bias_s
        p_s = masked_softmax(logit_s.reshape(B, G, HG, T, top_n * SEL_BLOCK), mask_s)
        o_s = jnp.einsum('bghtk,bgtkd->bghtd', p_s.astype(vs.dtype),
                         vs.reshape(B, G, T, top_n * SEL_BLOCK, HD))
        kw = lax.dynamic_slice_in_dim(k_win_p, i * T, WINDOW + T, axis=2)
        vw = lax.dynamic_slice_in_dim(v_win_p, i * T, WINDOW + T, axis=2)
        key_pos_w = i * T - WINDOW + jnp.arange(WINDOW + T)
        dist_w = q_pos[:, None] - key_pos_w[None, :]
        mask_w = (dist_w >= 0) & (dist_w < WINDOW) & (key_pos_w[None, :] >= 0)
        bias_w = tab[rel_bucket(dist_w)].transpose(2, 3, 0, 1)
        logit_w = jnp.einsum('bghtd,bgkd->bghtk', qi, kw) * scale + bias_w
        p_w = masked_softmax(logit_w, mask_w)
        o_w = jnp.einsum('bghtk,bgkd->bghtd', p_w.astype(vw.dtype), vw)
        gi = gi.astype(o_c.dtype)
        return (gi[..., 0:1] * o_c + gi[..., 1:2] * o_s + gi[..., 2:3] * o_w)

    o = lax.map(block, (qb, gb, jnp.arange(nb)))
    o = o.transpose(1, 0, 4, 2, 3, 5).reshape(B, S, NSA_Q_WIDTH)
    return o @ w_out


def peer_ffn(h, w_q, subkeys, u, v):
    B, S, D = h.shape
    T = PEER_TOKEN_BLOCK
    tokens = h.reshape((B * S) // T, T, D)
    half = PEER_QUERY_DIM // 2

    def block(xt):
        q = (xt @ w_q).reshape(T, PEER_HEADS, 2, half)
        s = jnp.einsum('thpd,pkd->thpk', q, subkeys).astype(jnp.float32)
        s1, i1 = lax.top_k(s[:, :, 0], PEER_TOPK)
        s2, i2 = lax.top_k(s[:, :, 1], PEER_TOPK)
        cand = (s1[..., :, None] + s2[..., None, :]).reshape(T, PEER_HEADS, PEER_TOPK * PEER_TOPK)
        cand_idx = (i1[..., :, None] * PEER_N_KEYS + i2[..., None, :]).reshape(T, PEER_HEADS, PEER_TOPK * PEER_TOPK)
        top_s, pos = lax.top_k(cand, PEER_TOPK)
        expert = jnp.take_along_axis(cand_idx, pos, axis=-1)
        gate = jax.nn.softmax(top_s, axis=-1)
        ue = u[expert]
        ve = v[expert]
        act = jax.nn.gelu(jnp.einsum('td,thkd->thk', xt, ue)).astype(jnp.float32)
        return jnp.einsum('thk,thkd->td', (gate * act).astype(ve.dtype), ve)

    return lax.map(block, tokens).reshape(B, S, D)


def setup_inputs(seed: int = 0) -> dict:
    key = jax.random.key(seed)
    ks = jax.random.split(key, 32)
    f32 = jnp.float32
    D = D_MODEL

    def nrm(k, shape, s):
        return jax.random.normal(k, shape, f32) * s

    def gain(k, shape):
        return 1.0 + 0.05 * jax.random.normal(k, shape, f32)

    return {
        'x': nrm(ks[0], (BATCH, SEQ, D), 1.0),
        'c': nrm(ks[1], (BATCH, D), 1.0),
        'rel_bias': nrm(ks[2], (REL_BUCKETS, N_HEADS), 0.5),
        'ada_w': nrm(ks[3], (DEPTH, D, N_ADA * D), 0.5 * D ** -0.5),
        'ada_b': nrm(ks[4], (DEPTH, N_ADA * D), 0.02),
        'norm_mix': gain(ks[5], (DEPTH, D)),
        'norm_ffn': gain(ks[6], (DEPTH, D)),
        'diff_w_in': nrm(ks[7], (N_A_LAYERS, D, 2 * DIFF_QK_WIDTH + DIFF_V_WIDTH), D ** -0.5),
        'diff_w_out': nrm(ks[8], (N_A_LAYERS, DIFF_V_WIDTH, D), DIFF_V_WIDTH ** -0.5),
        'diff_lambda': nrm(ks[9], (N_A_LAYERS, 4, DIFF_HEAD_DIM), 0.1),
        'diff_q_gain': gain(ks[10], (N_A_LAYERS, DIFF_HEAD_DIM)),
        'diff_k_gain': gain(ks[11], (N_A_LAYERS, DIFF_HEAD_DIM)),
        'diff_subln': gain(ks[12], (N_A_LAYERS, DIFF_V_DIM)),
        'kv_norm': gain(ks[13], (D,)),
        'kv_ada_w': nrm(ks[14], (D, 2 * D), 0.5 * D ** -0.5),
        'kv_ada_b': nrm(ks[15], (2 * D,), 0.02),
        'kv_w': nrm(ks[16], (D, NSA_KV_WIDTH), D ** -0.5),
        'cmp_pos': nrm(ks[17], (2, CMP_LEN, NSA_HEAD_DIM), 0.1),
        'cmp_w1': nrm(ks[18], (2, CMP_LEN * NSA_HEAD_DIM, CMP_HIDDEN), (CMP_LEN * NSA_HEAD_DIM) ** -0.5),
        'cmp_b1': nrm(ks[19], (2, CMP_HIDDEN), 0.02),
        'cmp_w2': nrm(ks[20], (2, CMP_HIDDEN, NSA_HEAD_DIM), CMP_HIDDEN ** -0.5),
        'nsa_k_gain': gain(ks[21], (N_BRANCH, NSA_HEAD_DIM)),
        'nsa_w_in': nrm(ks[22], (N_B_LAYERS, D, NSA_Q_WIDTH + NSA_GATE_WIDTH), D ** -0.5),
        'nsa_b_gate': nrm(ks[23], (N_B_LAYERS, NSA_GATE_WIDTH), 0.1),
        'nsa_w_out': nrm(ks[24], (N_B_LAYERS, NSA_Q_WIDTH, D), NSA_Q_WIDTH ** -0.5),
        'nsa_q_gain': gain(ks[25], (N_B_LAYERS, NSA_HEAD_DIM)),
        'peer_w_q': nrm(ks[26], (DEPTH, D, PEER_HEADS * PEER_QUERY_DIM), D ** -0.5),
        'peer_subkeys': nrm(ks[27], (DEPTH, 2, PEER_N_KEYS, PEER_QUERY_DIM // 2), (PEER_QUERY_DIM // 2) ** -0.5),
        'peer_u': nrm(ks[28], (DEPTH, PEER_N_EXPERTS, D), D ** -0.5),
        'peer_v': nrm(ks[29], (DEPTH, PEER_N_EXPERTS, D), 1.0),
    }


def reference(x, c, rel_bias, ada_w, ada_b, norm_mix, norm_ffn, diff_w_in, diff_w_out, diff_lambda,
              diff_q_gain, diff_k_gain, diff_subln, kv_norm, kv_ada_w, kv_ada_b, kv_w, cmp_pos, cmp_w1,
              cmp_b1, cmp_w2, nsa_k_gain, nsa_w_in, nsa_b_gate, nsa_w_out, nsa_q_gain, peer_w_q,
              peer_subkeys, peer_u, peer_v):
    c_act = jax.nn.silu(c)
    kv_shared = None
    for layer in range(DEPTH):
        if layer == N_A_LAYERS:
            kv_shared = nsa_shared_kv(x, c_act, kv_norm, kv_ada_w, kv_ada_b, kv_w, cmp_pos,
                                      cmp_w1, cmp_b1, cmp_w2, nsa_k_gain)
        sh_m, sc_m, g_m, sh_f, sc_f, g_f = jnp.split(c_act @ ada_w[layer] + ada_b[layer], N_ADA, axis=-1)
        h = modulate(rms_norm(x, norm_mix[layer]), sh_m, sc_m)
        if layer < N_A_LAYERS:
            y = diff_attention(h, diff_w_in[layer], diff_w_out[layer], diff_lambda[layer],
                               diff_q_gain[layer], diff_k_gain[layer], diff_subln[layer], rel_bias, layer)
        else:
            j = layer - N_A_LAYERS
            y = nsa_attention(h, nsa_w_in[j], nsa_b_gate[j], nsa_w_out[j], nsa_q_gain[j], rel_bias, *kv_shared)
        x = x + g_m[:, None, :] * y
        h = modulate(rms_norm(x, norm_ffn[layer]), sh_f, sc_f)
        x = x + g_f[:, None, :] * peer_ffn(h, peer_w_q[layer], peer_subkeys[layer], peer_u[layer], peer_v[layer])
    return x
```

```python
import functools
import math

import jax
import jax.numpy as jnp
import numpy as np
from jax import lax
from jax.experimental import pallas as pl
from jax.experimental.pallas import tpu as pltpu

F32 = jnp.float32
BF16 = jnp.bfloat16

N_HEADS = 8
N_ADA = 6
EPS = 1e-6
NEG_INF = -1e30
FORCE = 1e9

DIFF_HEAD_DIM = 64
HEAD_BLOCK = 128

NSA_KV_GROUPS = 2
NSA_HEADS_PER_GROUP = N_HEADS // NSA_KV_GROUPS
N_BRANCH = 3
CMP_LEN = 32
CMP_STRIDE = 16
CMP_HIDDEN = 256
SEL_BLOCK = 64
SEL_TOPN = 16
SEL_N_LOCAL = 2
WINDOW = 512
GATE_ROWS = 16

REL_BUCKETS = 32
REL_MAX_DIST = 1024

PEER_HEADS = 8
PEER_N_KEYS = 128
PEER_TOPK = 16
PEER_SLOTS = PEER_HEADS * PEER_TOPK

ROW_TILE = 512
ATT_TILE = 256
ROUTE_TILE = 256
PEER_TILE = 256
PEER_A_CHUNK = 4
VMEM_LIMIT = 56 * 1024 * 1024

NT_DIMS = (((1,), (1,)), ((), ()))


def _params(*sem):
    return pltpu.CompilerParams(dimension_semantics=sem, vmem_limit_bytes=VMEM_LIMIT)


def _nt_dot(a, b):
    return lax.dot_general(a, b, NT_DIMS, preferred_element_type=F32)


def _dot(a, b):
    return jnp.dot(a, b, preferred_element_type=F32)


def _rel_bucket(dist):
    max_exact = REL_BUCKETS // 2
    n = jnp.maximum(dist, 0)
    nf = jnp.maximum(n, 1).astype(jnp.float32)
    large = max_exact + (jnp.log(nf / max_exact) / math.log(REL_MAX_DIST / max_exact)
                         * (REL_BUCKETS - max_exact)).astype(jnp.int32)
    large = jnp.minimum(large, REL_BUCKETS - 1)
    return jnp.where(n < max_exact, n, large)


def _norm_mod(x, gain, shift, scale):
    y = x * lax.rsqrt(jnp.mean(x * x, axis=-1, keepdims=True) + EPS) * gain
    return y * (1.0 + scale) + shift


def _ada_kernel(c_ref, w_ref, b_ref, o_ref):
    c = c_ref[...]
    ca = c * jax.nn.sigmoid(c)
    o_ref[0] = jnp.dot(ca, w_ref[0], preferred_element_type=F32,
                       precision=lax.Precision.HIGHEST) + b_ref[0]


def _ada(c_pad, w, b, col_tile):
    layers, d, n = w.shape
    rows = c_pad.shape[0]
    return pl.pallas_call(
        _ada_kernel,
        grid=(layers, n // col_tile),
        in_specs=[
            pl.BlockSpec((rows, d), lambda l, j: (0, 0)),
            pl.BlockSpec((1, d, col_tile), lambda l, j: (l, 0, j)),
            pl.BlockSpec((1, 1, col_tile), lambda l, j: (l, 0, j)),
        ],
        out_specs=pl.BlockSpec((1, rows, col_tile), lambda l, j: (l, 0, j)),
        out_shape=jax.ShapeDtypeStruct((layers, rows, n), F32),
        compiler_params=_params("arbitrary", "arbitrary"),
        name="ada",
    )(c_pad, w, b.reshape(layers, 1, n))


def _diff_proj_kernel(x_ref, gain_ref, shift_ref, scale_ref, wqk_ref, wvt_ref, grp_ref,
                      qg_ref, kg_ref, q_ref, k_ref, vt_ref):
    h = _norm_mod(x_ref[...], gain_ref[...], shift_ref[0], scale_ref[0]).astype(BF16)
    qk = _dot(h, wqk_ref[...])
    width = N_HEADS * HEAD_BLOCK
    for part, out_ref, g_ref in ((0, q_ref, qg_ref), (1, k_ref, kg_ref)):
        z = qk[:, part * width:(part + 1) * width]
        zz = z * z
        hi = zz.astype(BF16)
        lo = (zz - hi.astype(F32)).astype(BF16)
        ss = _dot(hi, grp_ref[...]) + _dot(lo, grp_ref[...])
        zn = z * lax.rsqrt(ss * (1.0 / DIFF_HEAD_DIM) + EPS) * g_ref[...]
        for c in range(N_HEADS):
            out_ref[c] = zn[:, c * HEAD_BLOCK:(c + 1) * HEAD_BLOCK].astype(BF16)
    vt = _nt_dot(wvt_ref[...], h)
    n_chunk = vt.shape[1] // ATT_TILE
    for c in range(N_HEADS):
        for j in range(n_chunk):
            vt_ref[c, j] = vt[c * HEAD_BLOCK:(c + 1) * HEAD_BLOCK,
                              j * ATT_TILE:(j + 1) * ATT_TILE].astype(BF16)


def _diff_proj(x2, gain, shift, scale, wqk, wvt, grp, qg, kg, seq):
    n, d = x2.shape
    tm = ROW_TILE
    tiles_per_batch = seq // tm
    width = N_HEADS * HEAD_BLOCK
    const2 = lambda i: (0, 0)
    mod_spec = pl.BlockSpec((1, 1, d), lambda i: (i // tiles_per_batch, 0, 0))
    return pl.pallas_call(
        _diff_proj_kernel,
        grid=(n // tm,),
        in_specs=[
            pl.BlockSpec((tm, d), lambda i: (i, 0)),
            pl.BlockSpec((1, d), const2),
            mod_spec, mod_spec,
            pl.BlockSpec((d, 2 * width), const2),
            pl.BlockSpec((width, d), const2),
            pl.BlockSpec((width, width), const2),
            pl.BlockSpec((1, width), const2),
            pl.BlockSpec((1, width), const2),
        ],
        out_specs=[
            pl.BlockSpec((N_HEADS, tm, HEAD_BLOCK), lambda i: (0, i, 0)),
            pl.BlockSpec((N_HEADS, tm, HEAD_BLOCK), lambda i: (0, i, 0)),
            pl.BlockSpec((N_HEADS, tm // ATT_TILE, HEAD_BLOCK, ATT_TILE), lambda i: (0, i, 0, 0)),
        ],
        out_shape=[
            jax.ShapeDtypeStruct((N_HEADS, n, HEAD_BLOCK), BF16),
            jax.ShapeDtypeStruct((N_HEADS, n, HEAD_BLOCK), BF16),
            jax.ShapeDtypeStruct((N_HEADS, n // ATT_TILE, HEAD_BLOCK, ATT_TILE), BF16),
        ],
        compiler_params=_params("arbitrary"),
        name="diff_proj",
    )(x2, gain, shift, scale, wqk, wvt, grp, qg, kg)


def _online_softmax_step(s, vt, m_ref, l_ref, acc_ref):
    m_old = m_ref[...]
    m_new = jnp.maximum(m_old, jnp.max(s, axis=0, keepdims=True))
    alpha = jnp.exp(m_old - m_new)
    p = jnp.exp(s - m_new)
    l_ref[...] = alpha * l_ref[...] + jnp.sum(p, axis=0, keepdims=True)
    acc_ref[...] = alpha * acc_ref[...] + _dot(vt, p.astype(BF16))
    m_ref[...] = m_new


def _softmax_init(m_ref, l_ref, acc_ref):
    m_ref[...] = jnp.full(m_ref.shape, NEG_INF, F32)
    l_ref[...] = jnp.zeros(l_ref.shape, F32)
    acc_ref[...] = jnp.zeros(acc_ref.shape, F32)


def _diff_attn_kernel(lam_init, n_bias, q_ref, k_ref, vt_ref, bias_ref, lam_ref, sub_ref,
                      o_ref, m_ref, l_ref, acc_ref):
    qi = pl.program_id(2)
    t = ATT_TILE
    q = q_ref[0]
    lane = lax.broadcasted_iota(jnp.int32, q.shape, 1)
    zero = jnp.zeros_like(q)
    qs = jnp.concatenate([jnp.where(lane < DIFF_HEAD_DIM, q, zero),
                          jnp.where(lane >= DIFF_HEAD_DIM, q, zero)], axis=0)
    _softmax_init(m_ref, l_ref, acc_ref)

    def body(kt, carry):
        k = k_ref[0, pl.ds(pl.multiple_of(kt * t, t), t), :]
        bias = bias_ref[0, jnp.minimum(qi - kt, n_bias - 1)]
        s = _nt_dot(k, qs) + jnp.concatenate([bias, bias], axis=1)
        _online_softmax_step(s, vt_ref[0, kt], m_ref, l_ref, acc_ref)
        return carry

    lax.fori_loop(0, qi + 1, body, 0)

    lam = lam_ref[...]
    lam_full = (jnp.exp(jnp.sum(lam[0:1] * lam[1:2], axis=1, keepdims=True))
                - jnp.exp(jnp.sum(lam[2:3] * lam[3:4], axis=1, keepdims=True)) + lam_init)
    o = acc_ref[...] / l_ref[...]
    o = o[:, :t] - lam_full * o[:, t:]
    y = o * lax.rsqrt(jnp.mean(o * o, axis=0, keepdims=True) + EPS) * sub_ref[...]
    o_ref[...] = y.T.astype(BF16)


def _diff_attn(q, k, vt, bias, lam, sub_col, batch, seq, lam_init):
    t = ATT_TILE
    nq = seq // t
    n_bias = bias.shape[1]
    n = batch * seq
    return pl.pallas_call(
        functools.partial(_diff_attn_kernel, lam_init, n_bias),
        grid=(batch, N_HEADS, nq),
        in_specs=[
            pl.BlockSpec((1, t, HEAD_BLOCK), lambda b, h, i: (h, b * nq + i, 0)),
            pl.BlockSpec((1, seq, HEAD_BLOCK), lambda b, h, i: (h, b, 0)),
            pl.BlockSpec((1, nq, HEAD_BLOCK, t), lambda b, h, i: (h, b, 0, 0)),
            pl.BlockSpec((1, n_bias, t, t), lambda b, h, i: (h, 0, 0, 0)),
            pl.BlockSpec(lam.shape, lambda b, h, i: (0, 0)),
            pl.BlockSpec(sub_col.shape, lambda b, h, i: (0, 0)),
        ],
        out_specs=pl.BlockSpec((t, HEAD_BLOCK), lambda b, h, i: (b * nq + i, h)),
        out_shape=jax.ShapeDtypeStruct((n, N_HEADS * HEAD_BLOCK), BF16),
        scratch_shapes=[
            pltpu.VMEM((1, 2 * t), F32),
            pltpu.VMEM((1, 2 * t), F32),
            pltpu.VMEM((HEAD_BLOCK, 2 * t), F32),
        ],
        compiler_params=_params("arbitrary", "arbitrary", "arbitrary"),
        name="diff_attn",
    )(q, k, vt, bias, lam, sub_col)


def _out_proj_kernel(x_ref, o_ref, w_ref, g_ref, y_ref):
    y_ref[...] = x_ref[...] + g_ref[0] * _dot(o_ref[...], w_ref[...])


def _out_proj(x2, o, w, gate, seq):
    n, d = x2.shape
    tm = ROW_TILE
    tiles_per_batch = seq // tm
    return pl.pallas_call(
        _out_proj_kernel,
        grid=(n // tm,),
        in_specs=[
            pl.BlockSpec((tm, d), lambda i: (i, 0)),
            pl.BlockSpec((tm, o.shape[1]), lambda i: (i, 0)),
            pl.BlockSpec(w.shape, lambda i: (0, 0)),
            pl.BlockSpec((1, 1, d), lambda i: (i // tiles_per_batch, 0, 0)),
        ],
        out_specs=pl.BlockSpec((tm, d), lambda i: (i, 0)),
        out_shape=jax.ShapeDtypeStruct((n, d), F32),
        compiler_params=_params("arbitrary"),
        name="out_proj",
    )(x2, o, w, gate)


def _top_rows(s, count, val_ref, idx_ref, extra=()):
    rows = s.shape[0]
    iota = lax.broadcasted_iota(jnp.int32, s.shape, 0)
    for r in range(count):
        m = jnp.max(s, axis=0, keepdims=True)
        idx = jnp.min(jnp.where(s == m, iota, rows), axis=0, keepdims=True)
        hit = iota == idx
        val_ref[pl.ds(r, 1), :] = m
        if idx_ref is not None:
            idx_ref[pl.ds(r, 1), :] = idx
        for table, ref in extra:
            ref[pl.ds(r, 1), :] = jnp.max(jnp.where(hit, table, -1), axis=0, keepdims=True)
        s = jnp.where(hit, -jnp.inf, s)


def _peer_route_kernel(x_ref, gain_ref, shift_ref, scale_ref, wqt_ref, sk_ref,
                       ia_ref, ib_ref, gate_ref,
                       qt_ref, v1_ref, i1_ref, v2_ref, i2_ref, tv_ref, ta_ref, tb_ref):
    h = _norm_mod(x_ref[...], gain_ref[...], shift_ref[0], scale_ref[0]).astype(BF16)
    qt_ref[...] = _nt_dot(wqt_ref[...], h)
    k = PEER_TOPK
    half = PEER_N_KEYS

    def head(hd, carry):
        base = pl.multiple_of(hd * 2 * half, 2 * half)
        s1 = _dot(sk_ref[0], qt_ref[pl.ds(base, half), :].astype(BF16))
        s2 = _dot(sk_ref[1], qt_ref[pl.ds(base + half, half), :].astype(BF16))
        _top_rows(s1, k, v1_ref, i1_ref)
        _top_rows(s2, k, v2_ref, i2_ref)
        v1, i1, v2, i2 = v1_ref[...], i1_ref[...], v2_ref[...], i2_ref[...]
        tok = v1.shape[1]
        cand = jnp.concatenate([v1[a:a + 1] + v2 for a in range(k)], axis=0)
        cand_a = jnp.concatenate([jnp.broadcast_to(i1[a:a + 1], (k, tok)) for a in range(k)], axis=0)
        cand_b = jnp.concatenate([i2] * k, axis=0)
        _top_rows(cand, k, tv_ref, None, extra=((cand_a, ta_ref), (cand_b, tb_ref)))
        top = tv_ref[...]
        e = jnp.exp(top - jnp.max(top, axis=0, keepdims=True))
        gate = e / jnp.sum(e, axis=0, keepdims=True)
        row = pl.multiple_of(hd * k, k)
        ia_ref[pl.ds(row, k), :] = ta_ref[...]
        ib_ref[pl.ds(row, k), :] = tb_ref[...]
        gate_ref[pl.ds(row, k), :] = gate
        return carry

    lax.fori_loop(0, PEER_HEADS, head, 0)


def _peer_route(x2, gain, shift, scale, wqt, sk, seq):
    n, d = x2.shape
    t = ROUTE_TILE
    tiles_per_batch = seq // t
    const2 = lambda i: (0, 0)
    mod_spec = pl.BlockSpec((1, 1, d), lambda i: (i // tiles_per_batch, 0, 0))
    slot_spec = pl.BlockSpec((PEER_SLOTS, t), lambda i: (0, i))
    k = PEER_TOPK
    return pl.pallas_call(
        _peer_route_kernel,
        grid=(n // t,),
        in_specs=[
            pl.BlockSpec((t, d), lambda i: (i, 0)),
            pl.BlockSpec((1, d), const2),
            mod_spec, mod_spec,
            pl.BlockSpec(wqt.shape, const2),
            pl.BlockSpec(sk.shape, lambda i: (0, 0, 0)),
        ],
        out_specs=[slot_spec, slot_spec, slot_spec],
        out_shape=[
            jax.ShapeDtypeStruct((PEER_SLOTS, n), jnp.int32),
            jax.ShapeDtypeStruct((PEER_SLOTS, n), jnp.int32),
            jax.ShapeDtypeStruct((PEER_SLOTS, n), F32),
        ],
        scratch_shapes=[
            pltpu.VMEM((wqt.shape[0], t), F32),
            pltpu.VMEM((k, t), F32), pltpu.VMEM((k, t), jnp.int32),
            pltpu.VMEM((k, t), F32), pltpu.VMEM((k, t), jnp.int32),
            pltpu.VMEM((k, t), F32), pltpu.VMEM((k, t), jnp.int32), pltpu.VMEM((k, t), jnp.int32),
        ],
        compiler_params=_params("arbitrary"),
        name="peer_route",
    )(x2, gain, shift, scale, wqt, sk)


def _peer_dense_kernel(x_ref, gain_ref, shift_ref, scale_ref, gres_ref, ia_ref, ib_ref, gate_ref,
                       u_ref, v_ref, y_ref, w_ref, h_ref, acc_ref, iat_ref, ibt_ref, gt_ref):
    j = pl.program_id(1)
    t = x_ref.shape[0]
    nk = PEER_N_KEYS

    @pl.when(j == 0)
    def _():
        h_ref[...] = _norm_mod(x_ref[...], gain_ref[...], shift_ref[0], scale_ref[0]).astype(BF16)
        acc_ref[...] = jnp.zeros(acc_ref.shape, F32)
        iat_ref[...] = ia_ref[...].T
        ibt_ref[...] = ib_ref[...].T
        gt_ref[...] = gate_ref[...].T
        key = lax.broadcasted_iota(jnp.int32, (nk, PEER_SLOTS), 0)

        def build(tok, carry):
            a_hot = jnp.where(key == iat_ref[pl.ds(tok, 1), :], 1.0, 0.0).astype(BF16)
            b_gate = jnp.where(key == ibt_ref[pl.ds(tok, 1), :],
                               gt_ref[pl.ds(tok, 1), :], 0.0).astype(BF16)
            w_ref[pl.ds(pl.multiple_of(tok * nk, nk), nk), :] = _nt_dot(a_hot, b_gate)
            return carry

        lax.fori_loop(0, t, build, 0)

    act = _nt_dot(h_ref[...], u_ref[...])
    parts = []
    for al in range(PEER_A_CHUNK):
        w_a = w_ref[pl.ds(j * PEER_A_CHUNK + al, t, stride=nk), :]
        parts.append((w_a * jax.nn.gelu(act[:, al * nk:(al + 1) * nk])).astype(BF16))
    acc_ref[...] += _dot(jnp.concatenate(parts, axis=1), v_ref[...])

    @pl.when(j == pl.num_programs(1) - 1)
    def _():
        y_ref[...] = x_ref[...] + gres_ref[0] * acc_ref[...]


def _peer_dense(x2, gain, shift, scale, gres, ia, ib, gate, u, v, seq):
    n, d = x2.shape
    t = PEER_TILE
    tiles_per_batch = seq // t
    ec = PEER_A_CHUNK * PEER_N_KEYS
    n_exp = u.shape[0]
    mod_spec = pl.BlockSpec((1, 1, d), lambda i, j: (i // tiles_per_batch, 0, 0))
    slot_spec = pl.BlockSpec((PEER_SLOTS, t), lambda i, j: (0, i))
    return pl.pallas_call(
        _peer_dense_kernel,
        grid=(n // t, n_exp // ec),
        in_specs=[
            pl.BlockSpec((t, d), lambda i, j: (i, 0)),
            pl.BlockSpec((1, d), lambda i, j: (0, 0)),
            mod_spec, mod_spec, mod_spec,
            slot_spec, slot_spec, slot_spec,
            pl.BlockSpec((ec, d), lambda i, j: (j, 0)),
            pl.BlockSpec((ec, d), lambda i, j: (j, 0)),
        ],
        out_specs=pl.BlockSpec((t, d), lambda i, j: (i, 0)),
        out_shape=jax.ShapeDtypeStruct((n, d), F32),
        scratch_shapes=[
            pltpu.VMEM((t * PEER_N_KEYS, PEER_N_KEYS), F32),
            pltpu.VMEM((t, d), BF16),
            pltpu.VMEM((t, d), F32),
            pltpu.VMEM((t, PEER_SLOTS), jnp.int32),
            pltpu.VMEM((t, PEER_SLOTS), jnp.int32),
            pltpu.VMEM((t, PEER_SLOTS), F32),
        ],
        compiler_params=_params("arbitrary", "arbitrary"),
        name="peer_dense",
    )(x2, gain, shift, scale, gres, ia, ib, gate, u, v)


def _peer_layer(x2, gain, shift, scale, gres, wqt, sk, u, v, seq):
    ia, ib, gate = _peer_route(x2, gain, shift, scale, wqt, sk, seq)
    return _peer_dense(x2, gain, shift, scale, gres, ia, ib, gate, u, v, seq)


def _kv_proj_kernel(x_ref, gain_ref, shift_ref, scale_ref, wn_ref, wvt_ref, kg_ref,
                    cmp_ref, k_ref, vt_ref):
    h = _norm_mod(x_ref[...], gain_ref[...], shift_ref[0], scale_ref[0]).astype(BF16)
    kvn = _dot(h, wn_ref[...])
    hb = HEAD_BLOCK
    for c in range(4):
        cmp_ref[c] = kvn[:, c * hb:(c + 1) * hb].astype(BF16)
    for c in range(4):
        kk = kvn[:, (4 + c) * hb:(5 + c) * hb]
        kn = kk * lax.rsqrt(jnp.mean(kk * kk, axis=-1, keepdims=True) + EPS) * kg_ref[pl.ds(c // 2, 1), :]
        k_ref[c] = kn.astype(BF16)
    vt = _nt_dot(wvt_ref[...], h)
    for c in range(4):
        for j in range(vt.shape[1] // ATT_TILE):
            vt_ref[c, j] = vt[c * hb:(c + 1) * hb, j * ATT_TILE:(j + 1) * ATT_TILE].astype(BF16)


def _kv_proj(x2, gain, shift, scale, wn, wvt, kg, seq):
    n, d = x2.shape
    tm = ROW_TILE
    tiles_per_batch = seq // tm
    const2 = lambda i: (0, 0)
    mod_spec = pl.BlockSpec((1, 1, d), lambda i: (i // tiles_per_batch, 0, 0))
    return pl.pallas_call(
        _kv_proj_kernel,
        grid=(n // tm,),
        in_specs=[
            pl.BlockSpec((tm, d), lambda i: (i, 0)),
            pl.BlockSpec((1, d), const2),
            mod_spec, mod_spec,
            pl.BlockSpec(wn.shape, const2),
            pl.BlockSpec(wvt.shape, const2),
            pl.BlockSpec(kg.shape, const2),
        ],
        out_specs=[
            pl.BlockSpec((4, tm, HEAD_BLOCK), lambda i: (0, i, 0)),
            pl.BlockSpec((4, tm, HEAD_BLOCK), lambda i: (0, i, 0)),
            pl.BlockSpec((4, tm // ATT_TILE, HEAD_BLOCK, ATT_TILE), lambda i: (0, i, 0, 0)),
        ],
        out_shape=[
            jax.ShapeDtypeStruct((4, n, HEAD_BLOCK), BF16),
            jax.ShapeDtypeStruct((4, n, HEAD_BLOCK), BF16),
            jax.ShapeDtypeStruct((4, n // ATT_TILE, HEAD_BLOCK, ATT_TILE), BF16),
        ],
        compiler_params=_params("arbitrary"),
        name="kv_proj",
    )(x2, gain, shift, scale, wn, wvt, kg)


def _compress_kernel(xk_ref, xv_ref, w1_ref, pos_ref, b1_ref, w2_ref, kg_ref, kc_ref, vct_ref):
    half = w1_ref.shape[1] // 2
    rows = xk_ref.shape[2]
    for kv, x_ref in ((0, xk_ref), (1, xv_ref)):
        x = x_ref[0, 0]
        w1 = w1_ref[kv]
        first = _dot(x, w1[:half])
        second = _dot(x, w1[half:])
        hid = first + pltpu.roll(second, rows - 1, axis=0)
        pos_term = _dot(jnp.broadcast_to(pos_ref[kv], (8, 2 * half)).astype(BF16), w1)[0:1]
        hid = jax.nn.gelu(hid + pos_term + b1_ref[kv])
        out = _dot(hid.astype(BF16), w2_ref[kv])
        if kv == 0:
            out = out * lax.rsqrt(jnp.mean(out * out, axis=-1, keepdims=True) + EPS) * kg_ref[...]
            kc_ref[0, 0] = out.astype(BF16)
        else:
            vct_ref[0, 0] = out.T.astype(BF16)


def _compress(xr, w1, pos, b1, w2, kg, batch):
    _, _, rows, width = xr.shape
    g = NSA_KV_GROUPS
    return pl.pallas_call(
        _compress_kernel,
        grid=(batch, g),
        in_specs=[
            pl.BlockSpec((1, 1, rows, width), lambda b, gi: (gi, b, 0, 0)),
            pl.BlockSpec((1, 1, rows, width), lambda b, gi: (g + gi, b, 0, 0)),
            pl.BlockSpec(w1.shape, lambda b, gi: (0, 0, 0)),
            pl.BlockSpec(pos.shape, lambda b, gi: (0, 0, 0)),
            pl.BlockSpec(b1.shape, lambda b, gi: (0, 0, 0)),
            pl.BlockSpec(w2.shape, lambda b, gi: (0, 0, 0)),
            pl.BlockSpec(kg.shape, lambda b, gi: (0, 0)),
        ],
        out_specs=[
            pl.BlockSpec((1, 1, rows, HEAD_BLOCK), lambda b, gi: (b, gi, 0, 0)),
            pl.BlockSpec((1, 1, HEAD_BLOCK, rows), lambda b, gi: (b, gi, 0, 0)),
        ],
        out_shape=[
            jax.ShapeDtypeStruct((batch, g, rows, HEAD_BLOCK), BF16),
            jax.ShapeDtypeStruct((batch, g, HEAD_BLOCK, rows), BF16),
        ],
        compiler_params=_params("arbitrary", "arbitrary"),
        name="compress",
    )(xr, xr, w1, pos, b1, w2, kg)


def _nsa_q_proj_kernel(x_ref, gain_ref, shift_ref, scale_ref, wq_ref, wgt_ref, bg_ref, qg_ref,
                       q_ref, gt_ref):
    h = _norm_mod(x_ref[...], gain_ref[...], shift_ref[0], scale_ref[0]).astype(BF16)
    q = _dot(h, wq_ref[...])
    hb = HEAD_BLOCK
    for c in range(N_HEADS):
        qq = q[:, c * hb:(c + 1) * hb]
        qn = qq * lax.rsqrt(jnp.mean(qq * qq, axis=-1, keepdims=True) + EPS) * qg_ref[...]
        q_ref[c] = qn.astype(BF16)
    gt_ref[...] = jax.nn.sigmoid(_nt_dot(wgt_ref[...], h) + bg_ref[...])


def _nsa_q_proj(x2, gain, shift, scale, wq, wgt, bg, qg, seq):
    n, d = x2.shape
    tm = ROW_TILE
    tiles_per_batch = seq // tm
    const2 = lambda i: (0, 0)
    mod_spec = pl.BlockSpec((1, 1, d), lambda i: (i // tiles_per_batch, 0, 0))
    rows = wgt.shape[0]
    return pl.pallas_call(
        _nsa_q_proj_kernel,
        grid=(n // tm,),
        in_specs=[
            pl.BlockSpec((tm, d), lambda i: (i, 0)),
            pl.BlockSpec((1, d), const2),
            mod_spec, mod_spec,
            pl.BlockSpec(wq.shape, const2),
            pl.BlockSpec(wgt.shape, const2),
            pl.BlockSpec(bg.shape, const2),
            pl.BlockSpec(qg.shape, const2),
        ],
        out_specs=[
            pl.BlockSpec((N_HEADS, tm, HEAD_BLOCK), lambda i: (0, i, 0)),
            pl.BlockSpec((rows, tm), lambda i: (0, i)),
        ],
        out_shape=[
            jax.ShapeDtypeStruct((N_HEADS, n, HEAD_BLOCK), BF16),
            jax.ShapeDtypeStruct((rows, n), F32),
        ],
        compiler_params=_params("arbitrary"),
        name="nsa_q_proj",
    )(x2, gain, shift, scale, wq, wgt, bg, qg)


def _nsa_attn_kernel(n_slc_bias, n_win_bias, top_n,
                     q_ref, g_ref, kc_ref, vct_ref, ks_ref, vst_ref, kw_ref, vwt_ref,
                     cb_ref, sb_ref, wb_ref, ov_ref, o_ref,
                     sel_ref, oc_ref, m_ref, l_ref, acc_ref, os_ref):
    qi = pl.program_id(2)
    nq = pl.num_programs(2)
    t = ATT_TILE
    hg = NSA_HEADS_PER_GROUP
    q = q_ref[...].reshape(hg * t, HEAD_BLOCK)
    n_cmp_rows = kc_ref.shape[2]
    n_blk = ov_ref.shape[0]

    off = pl.multiple_of((nq - 1 - qi) * (t // CMP_STRIDE), t // CMP_STRIDE)
    bias_c = cb_ref[0, pl.ds(off, n_cmp_rows), :]
    sc = _nt_dot(kc_ref[0, 0], q) + bias_c
    visible = bias_c > 0.5 * NEG_INF
    e = jnp.where(visible, jnp.exp(sc - jnp.max(sc, axis=0, keepdims=True)), 0.0)
    lsum = jnp.sum(e, axis=0, keepdims=True)
    p = e / jnp.where(lsum > 0.0, lsum, 1.0)
    oc_ref[...] = _dot(vct_ref[0, 0], p.astype(BF16))

    psum = p[:, 0:t]
    for i in range(1, hg):
        psum = psum + p[:, i * t:(i + 1) * t]
    p_hi = psum.astype(BF16)
    p_lo = (psum - p_hi.astype(F32)).astype(BF16)
    imp = _dot(ov_ref[...], p_hi) + _dot(ov_ref[...], p_lo)
    blk = lax.broadcasted_iota(jnp.int32, imp.shape, 0)
    pos = qi * t + lax.broadcasted_iota(jnp.int32, imp.shape, 1)
    back = pos // SEL_BLOCK - blk
    forced = (blk == 0) | ((back >= 0) & (back < SEL_N_LOCAL))
    imp = jnp.where(forced, FORCE, imp)
    imp = jnp.where(back >= 0, imp, NEG_INF)
    sel = jnp.zeros(imp.shape, F32)
    for _ in range(top_n):
        m = jnp.max(imp, axis=0, keepdims=True)
        idx = jnp.min(jnp.where(imp == m, blk, n_blk), axis=0, keepdims=True)
        hit = blk == idx
        sel = jnp.where(hit & (m >= 0.0), 1.0, sel)
        imp = jnp.where(hit, -jnp.inf, imp)
    sel_ref[...] = jnp.where(sel > 0.0, 0.0, NEG_INF)

    blocks_per_tile = t // SEL_BLOCK

    _softmax_init(m_ref, l_ref, acc_ref)

    def slc_body(kt, carry):
        k = ks_ref[0, pl.ds(pl.multiple_of(kt * t, t), t), :]
        bias = sb_ref[0, jnp.minimum(qi - kt, n_slc_bias - 1)]
        rows = [jnp.broadcast_to(sel_ref[pl.ds(kt * blocks_per_tile + jj, 1), :], (SEL_BLOCK, t))
                for jj in range(blocks_per_tile)]
        mask = jnp.concatenate(rows, axis=0)
        s = _nt_dot(k, q) + bias + jnp.concatenate([mask] * hg, axis=1)
        _online_softmax_step(s, vst_ref[0, kt], m_ref, l_ref, acc_ref)
        return carry

    lax.fori_loop(0, qi + 1, slc_body, 0)
    os_ref[...] = acc_ref[...] / l_ref[...]

    _softmax_init(m_ref, l_ref, acc_ref)

    def win_body(kt, carry):
        k = kw_ref[0, pl.ds(pl.multiple_of(kt * t, t), t), :]
        s = _nt_dot(k, q) + wb_ref[0, qi - kt]
        _online_softmax_step(s, vwt_ref[0, kt], m_ref, l_ref, acc_ref)
        return carry

    lax.fori_loop(jnp.maximum(qi - (n_win_bias - 1), 0), qi + 1, win_body, 0)
    ow = acc_ref[...] / l_ref[...]

    gates = g_ref[...]
    for i in range(hg):
        sl = slice(i * t, (i + 1) * t)
        o = (gates[3 * i:3 * i + 1] * oc_ref[:, sl] + gates[3 * i + 1:3 * i + 2] * os_ref[:, sl]
             + gates[3 * i + 2:3 * i + 3] * ow[:, sl])
        o_ref[:, i * HEAD_BLOCK:(i + 1) * HEAD_BLOCK] = o.T.astype(BF16)


def _nsa_attn(q, gt, kc, vct, k, vt, cmp_bias, slc_bias, win_bias, ovt, batch, seq):
    t = ATT_TILE
    nq = seq // t
    hg = NSA_HEADS_PER_GROUP
    g = NSA_KV_GROUPS
    n = batch * seq
    top_n = min(SEL_TOPN, seq // SEL_BLOCK)
    lanes = hg * t
    idx3 = lambda b, gi, i: (gi, 0, 0)
    idx4 = lambda b, gi, i: (gi, 0, 0, 0)
    return pl.pallas_call(
        functools.partial(_nsa_attn_kernel, slc_bias.shape[1], win_bias.shape[1], top_n),
        grid=(batch, g, nq),
        in_specs=[
            pl.BlockSpec((hg, t, HEAD_BLOCK), lambda b, gi, i: (gi, b * nq + i, 0)),
            pl.BlockSpec((GATE_ROWS, t), lambda b, gi, i: (gi, b * nq + i)),
            pl.BlockSpec((1, 1) + kc.shape[2:], lambda b, gi, i: (b, gi, 0, 0)),
            pl.BlockSpec((1, 1) + vct.shape[2:], lambda b, gi, i: (b, gi, 0, 0)),
            pl.BlockSpec((1, seq, HEAD_BLOCK), lambda b, gi, i: (gi, b, 0)),
            pl.BlockSpec((1, nq, HEAD_BLOCK, t), lambda b, gi, i: (gi, b, 0, 0)),
            pl.BlockSpec((1, seq, HEAD_BLOCK), lambda b, gi, i: (g + gi, b, 0)),
            pl.BlockSpec((1, nq, HEAD_BLOCK, t), lambda b, gi, i: (g + gi, b, 0, 0)),
            pl.BlockSpec((1,) + cmp_bias.shape[1:], idx3),
            pl.BlockSpec((1,) + slc_bias.shape[1:], idx4),
            pl.BlockSpec((1,) + win_bias.shape[1:], idx4),
            pl.BlockSpec(ovt.shape, lambda b, gi, i: (0, 0)),
        ],
        out_specs=pl.BlockSpec((t, hg * HEAD_BLOCK), lambda b, gi, i: (b * nq + i, gi)),
        out_shape=jax.ShapeDtypeStruct((n, N_HEADS * HEAD_BLOCK), BF16),
        scratch_shapes=[
            pltpu.VMEM((ovt.shape[0], t), F32),
            pltpu.VMEM((HEAD_BLOCK, lanes), F32),
            pltpu.VMEM((1, lanes), F32),
            pltpu.VMEM((1, lanes), F32),
            pltpu.VMEM((HEAD_BLOCK, lanes), F32),
            pltpu.VMEM((HEAD_BLOCK, lanes), F32),
        ],
        compiler_params=_params("arbitrary", "arbitrary", "arbitrary"),
        name="nsa_attn",
    )(q, gt, kc, vct, k, vt, k, vt, cmp_bias, slc_bias, win_bias, ovt)


def _bias_tiles(profile, seq, n_tiles, window=None):
    t = ATT_TILE
    d = np.arange(n_tiles)[:, None, None]
    dist = d * t + np.arange(t)[None, None, :] - np.arange(t)[None, :, None]
    ok = dist >= 0
    if window is not None:
        ok &= dist < window
    tiles = profile[:, np.clip(dist, 0, seq - 1)]
    return jnp.where(jnp.asarray(ok)[None], tiles, NEG_INF)


def _n_far_tiles():
    return -(-(REL_MAX_DIST + ATT_TILE - 1) // ATT_TILE) + 1


def _group_lanes(tiles):
    hg = NSA_HEADS_PER_GROUP
    parts = tiles.reshape((NSA_KV_GROUPS, hg) + tiles.shape[1:])
    return jnp.concatenate([parts[:, i] for i in range(hg)], axis=-1)


def kernel(x, c, rel_bias, ada_w, ada_b, norm_mix, norm_ffn, diff_w_in, diff_w_out, diff_lambda,
           diff_q_gain, diff_k_gain, diff_subln, kv_norm, kv_ada_w, kv_ada_b, kv_w, cmp_pos, cmp_w1,
           cmp_b1, cmp_w2, nsa_k_gain, nsa_w_in, nsa_b_gate, nsa_w_out, nsa_q_gain, peer_w_q,
           peer_subkeys, peer_u, peer_v):
    batch, seq, d = x.shape
    n = batch * seq
    t = ATT_TILE
    nq = seq // t
    width = N_HEADS * HEAD_BLOCK
    x2 = x.reshape(n, d)

    c_pad = jnp.pad(c, ((0, 8 - batch % 8 if batch % 8 else 0), (0, 0)))
    mods = _ada(c_pad, ada_w, ada_b, 1536)
    kv_mods = _ada(c_pad, kv_ada_w[None], kv_ada_b[None], 1024)

    def mod(arr, layer, k):
        return arr[layer, :batch, k * d:(k + 1) * d].reshape(batch, 1, d)

    profile = rel_bias[_rel_bucket(jnp.arange(seq))].T
    n_far = min(_n_far_tiles(), nq)

    lam_init = 0.8 - 0.6 * math.exp(-0.3 * 0)
    w_in = diff_w_in[0]
    wqk = w_in[:, :2 * width].astype(BF16)
    wvt = w_in[:, 2 * width:].T.astype(BF16)
    grp = jnp.asarray(np.kron(np.eye(width // DIFF_HEAD_DIM), np.ones((DIFF_HEAD_DIM, DIFF_HEAD_DIM))), BF16)
    reps = width // DIFF_HEAD_DIM
    qg = (jnp.tile(diff_q_gain[0], reps) * DIFF_HEAD_DIM ** -0.5).reshape(1, width)
    kg = jnp.tile(diff_k_gain[0], reps).reshape(1, width)
    q, k, vt = _diff_proj(x2, norm_mix[0:1], mod(mods, 0, 0), mod(mods, 0, 1), wqk, wvt, grp, qg, kg, seq)
    diff_bias = _bias_tiles(profile, seq, n_far)
    sub_col = (diff_subln[0] * (1.0 - lam_init)).reshape(HEAD_BLOCK, 1)
    o = _diff_attn(q, k, vt, diff_bias, diff_lambda[0], sub_col, batch, seq, lam_init)
    x2 = _out_proj(x2, o, diff_w_out[0].astype(BF16), mod(mods, 0, 2), seq)

    def peer(x2, layer):
        return _peer_layer(x2, norm_ffn[layer:layer + 1], mod(mods, layer, 3), mod(mods, layer, 4),
                           mod(mods, layer, 5), peer_w_q[layer].T.astype(BF16),
                           peer_subkeys[layer].astype(BF16), peer_u[layer].astype(BF16),
                           peer_v[layer].astype(BF16), seq)

    x2 = peer(x2, 0)

    hb = HEAD_BLOCK

    def kv_cols(branch, kv, g):
        start = branch * 4 * hb + kv * 2 * hb + g * hb
        return kv_w[:, start:start + hb]

    wn = jnp.concatenate([kv_cols(0, 0, 0), kv_cols(0, 0, 1), kv_cols(0, 1, 0), kv_cols(0, 1, 1),
                          kv_cols(1, 0, 0), kv_cols(1, 0, 1), kv_cols(2, 0, 0), kv_cols(2, 0, 1)],
                         axis=1).astype(BF16)
    wvt_kv = jnp.concatenate([kv_cols(1, 1, 0), kv_cols(1, 1, 1), kv_cols(2, 1, 0), kv_cols(2, 1, 1)],
                             axis=1).T.astype(BF16)
    cmp_raw, k_sw, vt_sw = _kv_proj(x2, kv_norm.reshape(1, d), mod(kv_mods, 0, 0), mod(kv_mods, 0, 1),
                                    wn, wvt_kv, nsa_k_gain[1:3], seq)
    xr = cmp_raw.reshape(4, batch, seq // CMP_STRIDE, CMP_STRIDE * hb)
    kc, vct = _compress(xr, cmp_w1.astype(BF16), cmp_pos.reshape(2, 1, CMP_LEN * hb),
                        cmp_b1.reshape(2, 1, CMP_HIDDEN), cmp_w2.astype(BF16), nsa_k_gain[0:1], batch)

    hg = NSA_HEADS_PER_GROUP
    w_in1 = nsa_w_in[0]
    wq = w_in1[:, :width].astype(BF16)
    w_gate = w_in1[:, width:].reshape(d, NSA_KV_GROUPS, hg * N_BRANCH)
    w_gate = jnp.pad(w_gate, ((0, 0), (0, 0), (0, GATE_ROWS - hg * N_BRANCH)))
    wgt = w_gate.reshape(d, NSA_KV_GROUPS * GATE_ROWS).T.astype(BF16)
    b_gate = jnp.pad(nsa_b_gate[0].reshape(NSA_KV_GROUPS, hg * N_BRANCH),
                     ((0, 0), (0, GATE_ROWS - hg * N_BRANCH))).reshape(NSA_KV_GROUPS * GATE_ROWS, 1)
    qg1 = (nsa_q_gain[0] * hb ** -0.5).reshape(1, hb)
    q1, gt = _nsa_q_proj(x2, norm_mix[1:2], mod(mods, 1, 0), mod(mods, 1, 1), wq, wgt, b_gate, qg1, seq)

    slc_bias = _group_lanes(_bias_tiles(profile, seq, n_far))
    n_win = min(WINDOW // t + 1, nq)
    win_bias = _group_lanes(_bias_tiles(profile, seq, n_win, window=WINDOW))
    n_rows = seq // CMP_STRIDE
    strip_rows = n_rows + (nq - 1) * (t // CMP_STRIDE)
    dist_c = (seq - t) + np.arange(t)[None, :] - (CMP_STRIDE * np.arange(strip_rows)[:, None] + CMP_LEN - 1)
    strip = jnp.where(jnp.asarray(dist_c >= 0)[None], profile[:, np.clip(dist_c, 0, seq - 1)], NEG_INF)
    cmp_bias = _group_lanes(strip)
    n_sel = seq // SEL_BLOCK
    cmp_start = np.arange(n_rows) * CMP_STRIDE
    sel_start = np.arange(n_sel) * SEL_BLOCK
    overlap = np.clip(np.minimum(cmp_start[:, None] + CMP_LEN, sel_start[None, :] + SEL_BLOCK)
                      - np.maximum(cmp_start[:, None], sel_start[None, :]), 0, None) / CMP_LEN
    ovt = jnp.asarray(overlap.T, BF16)
    o1 = _nsa_attn(q1, gt, kc, vct, k_sw, vt_sw, cmp_bias, slc_bias, win_bias, ovt, batch, seq)
    x2 = _out_proj(x2, o1, nsa_w_out[0].astype(BF16), mod(mods, 1, 2), seq)

    x2 = peer(x2, 1)
    return x2.reshape(batch, seq, d)
```

```python
import functools
import math

import jax
import jax.numpy as jnp
import numpy as np
from jax import lax
from jax.experimental import pallas as pl
from jax.experimental.pallas import tpu as pltpu

F32 = jnp.float32
BF16 = jnp.bfloat16

N_HEADS = 8
N_ADA = 6
EPS = 1e-6
NEG_INF = -1e30
FORCE = 1e9

DIFF_HEAD_DIM = 64
HEAD_BLOCK = 128

NSA_KV_GROUPS = 2
NSA_HEADS_PER_GROUP = N_HEADS // NSA_KV_GROUPS
N_BRANCH = 3
CMP_LEN = 32
CMP_STRIDE = 16
CMP_HIDDEN = 256
SEL_BLOCK = 64
SEL_TOPN = 16
SEL_N_LOCAL = 2
WINDOW = 512
GATE_ROWS = 16

REL_BUCKETS = 32
REL_MAX_DIST = 1024

PEER_HEADS = 8
PEER_N_KEYS = 128
PEER_TOPK = 16
PEER_SLOTS = PEER_HEADS * PEER_TOPK

ROW_TILE = 512
ATT_TILE = 256
DIFF_TILE = 512
ROUTE_TILE = 256
PEER_TILE = 512
PEER_BUILD_GROUP = 16
PEER_A_CHUNK = 4
VMEM_LIMIT = 56 * 1024 * 1024

NT_DIMS = (((1,), (1,)), ((), ()))


def _params(*sem):
    return pltpu.CompilerParams(dimension_semantics=sem, vmem_limit_bytes=VMEM_LIMIT)


def _nt_dot(a, b):
    return lax.dot_general(a, b, NT_DIMS, preferred_element_type=F32)


def _dot(a, b):
    return jnp.dot(a, b, preferred_element_type=F32)


def _rel_bucket(dist):
    max_exact = REL_BUCKETS // 2
    n = jnp.maximum(dist, 0)
    nf = jnp.maximum(n, 1).astype(jnp.float32)
    large = max_exact + (jnp.log(nf / max_exact) / math.log(REL_MAX_DIST / max_exact)
                         * (REL_BUCKETS - max_exact)).astype(jnp.int32)
    large = jnp.minimum(large, REL_BUCKETS - 1)
    return jnp.where(n < max_exact, n, large)


def _norm_mod(x, gain, shift, scale):
    y = x * lax.rsqrt(jnp.mean(x * x, axis=-1, keepdims=True) + EPS) * gain
    return y * (1.0 + scale) + shift


def _ada_kernel(c_ref, w_ref, b_ref, o_ref):
    c = c_ref[...]
    ca = c * jax.nn.sigmoid(c)
    o_ref[0] = jnp.dot(ca, w_ref[0], preferred_element_type=F32,
                       precision=lax.Precision.HIGHEST) + b_ref[0]


def _ada(c_pad, w, b, col_tile):
    layers, d, n = w.shape
    rows = c_pad.shape[0]
    return pl.pallas_call(
        _ada_kernel,
        grid=(layers, n // col_tile),
        in_specs=[
            pl.BlockSpec((rows, d), lambda l, j: (0, 0)),
            pl.BlockSpec((1, d, col_tile), lambda l, j: (l, 0, j)),
            pl.BlockSpec((1, 1, col_tile), lambda l, j: (l, 0, j)),
        ],
        out_specs=pl.BlockSpec((1, rows, col_tile), lambda l, j: (l, 0, j)),
        out_shape=jax.ShapeDtypeStruct((layers, rows, n), F32),
        compiler_params=_params("arbitrary", "arbitrary"),
        name="ada",
    )(c_pad, w, b.reshape(layers, 1, n))


def _diff_proj_kernel(x_ref, gain_ref, shift_ref, scale_ref, wqk_ref, wvt_ref, grp_ref,
                      qg_ref, kg_ref, q_ref, k_ref, vt_ref):
    h = _norm_mod(x_ref[...], gain_ref[...], shift_ref[0], scale_ref[0]).astype(BF16)
    qk = _dot(h, wqk_ref[...])
    width = N_HEADS * HEAD_BLOCK
    for part, out_ref, g_ref in ((0, q_ref, qg_ref), (1, k_ref, kg_ref)):
        z = qk[:, part * width:(part + 1) * width]
        zz = z * z
        hi = zz.astype(BF16)
        lo = (zz - hi.astype(F32)).astype(BF16)
        ss = _dot(hi, grp_ref[...]) + _dot(lo, grp_ref[...])
        zn = z * lax.rsqrt(ss * (1.0 / DIFF_HEAD_DIM) + EPS) * g_ref[...]
        for c in range(N_HEADS):
            out_ref[c] = zn[:, c * HEAD_BLOCK:(c + 1) * HEAD_BLOCK].astype(BF16)
    vt = _nt_dot(wvt_ref[...], h)
    n_chunk = vt.shape[1] // DIFF_TILE
    for c in range(N_HEADS):
        for j in range(n_chunk):
            vt_ref[c, j] = vt[c * HEAD_BLOCK:(c + 1) * HEAD_BLOCK,
                              j * DIFF_TILE:(j + 1) * DIFF_TILE].astype(BF16)


def _diff_proj(x2, gain, shift, scale, wqk, wvt, grp, qg, kg, seq):
    n, d = x2.shape
    tm = ROW_TILE
    tiles_per_batch = seq // tm
    width = N_HEADS * HEAD_BLOCK
    const2 = lambda i: (0, 0)
    mod_spec = pl.BlockSpec((1, 1, d), lambda i: (i // tiles_per_batch, 0, 0))
    return pl.pallas_call(
        _diff_proj_kernel,
        grid=(n // tm,),
        in_specs=[
            pl.BlockSpec((tm, d), lambda i: (i, 0)),
            pl.BlockSpec((1, d), const2),
            mod_spec, mod_spec,
            pl.BlockSpec((d, 2 * width), const2),
            pl.BlockSpec((width, d), const2),
            pl.BlockSpec((width, width), const2),
            pl.BlockSpec((1, width), const2),
            pl.BlockSpec((1, width), const2),
        ],
        out_specs=[
            pl.BlockSpec((N_HEADS, tm, HEAD_BLOCK), lambda i: (0, i, 0)),
            pl.BlockSpec((N_HEADS, tm, HEAD_BLOCK), lambda i: (0, i, 0)),
            pl.BlockSpec((N_HEADS, tm // DIFF_TILE, HEAD_BLOCK, DIFF_TILE), lambda i: (0, i, 0, 0)),
        ],
        out_shape=[
            jax.ShapeDtypeStruct((N_HEADS, n, HEAD_BLOCK), BF16),
            jax.ShapeDtypeStruct((N_HEADS, n, HEAD_BLOCK), BF16),
            jax.ShapeDtypeStruct((N_HEADS, n // DIFF_TILE, HEAD_BLOCK, DIFF_TILE), BF16),
        ],
        compiler_params=_params("arbitrary"),
        name="diff_proj",
    )(x2, gain, shift, scale, wqk, wvt, grp, qg, kg)


def _online_softmax_step(s, vt, m_ref, l_ref, acc_ref):
    m_old = m_ref[...]
    m_new = jnp.maximum(m_old, jnp.max(s, axis=0, keepdims=True))
    alpha = jnp.exp(m_old - m_new)
    p = jnp.exp(s - m_new)
    l_ref[...] = alpha * l_ref[...] + jnp.sum(p, axis=0, keepdims=True)
    acc_ref[...] = alpha * acc_ref[...] + _dot(vt, p.astype(BF16))
    m_ref[...] = m_new


def _softmax_init(m_ref, l_ref, acc_ref):
    m_ref[...] = jnp.full(m_ref.shape, NEG_INF, F32)
    l_ref[...] = jnp.zeros(l_ref.shape, F32)
    acc_ref[...] = jnp.zeros(acc_ref.shape, F32)


def _diff_attn_kernel(lam_init, n_bias, q_ref, k_ref, vt_ref, bias_ref, lam_ref, sub_ref,
                      o_ref, m_ref, l_ref, acc_ref):
    qi = pl.program_id(2)
    t = DIFF_TILE
    q = q_ref[0]
    lane = lax.broadcasted_iota(jnp.int32, q.shape, 1)
    zero = jnp.zeros_like(q)
    qs = jnp.concatenate([jnp.where(lane < DIFF_HEAD_DIM, q, zero),
                          jnp.where(lane >= DIFF_HEAD_DIM, q, zero)], axis=0)
    _softmax_init(m_ref, l_ref, acc_ref)

    def body(kt, carry):
        k = k_ref[0, pl.ds(pl.multiple_of(kt * t, t), t), :]
        bias = bias_ref[0, jnp.minimum(qi - kt, n_bias - 1)]
        s = _nt_dot(k, qs) + jnp.concatenate([bias, bias], axis=1)
        _online_softmax_step(s, vt_ref[0, kt], m_ref, l_ref, acc_ref)
        return carry

    lax.fori_loop(0, qi + 1, body, 0)

    lam = lam_ref[...]
    lam_full = (jnp.exp(jnp.sum(lam[0:1] * lam[1:2], axis=1, keepdims=True))
                - jnp.exp(jnp.sum(lam[2:3] * lam[3:4], axis=1, keepdims=True)) + lam_init)
    o = acc_ref[...] / l_ref[...]
    o = o[:, :t] - lam_full * o[:, t:]
    y = o * lax.rsqrt(jnp.mean(o * o, axis=0, keepdims=True) + EPS) * sub_ref[...]
    o_ref[...] = y.T.astype(BF16)


def _diff_attn(q, k, vt, bias, lam, sub_col, batch, seq, lam_init):
    t = DIFF_TILE
    nq = seq // t
    n_bias = bias.shape[1]
    n = batch * seq
    return pl.pallas_call(
        functools.partial(_diff_attn_kernel, lam_init, n_bias),
        grid=(batch, N_HEADS, nq),
        in_specs=[
            pl.BlockSpec((1, t, HEAD_BLOCK), lambda b, h, i: (h, b * nq + i, 0)),
            pl.BlockSpec((1, seq, HEAD_BLOCK), lambda b, h, i: (h, b, 0)),
            pl.BlockSpec((1, nq, HEAD_BLOCK, t), lambda b, h, i: (h, b, 0, 0)),
            pl.BlockSpec((1, n_bias, t, t), lambda b, h, i: (h, 0, 0, 0)),
            pl.BlockSpec(lam.shape, lambda b, h, i: (0, 0)),
            pl.BlockSpec(sub_col.shape, lambda b, h, i: (0, 0)),
        ],
        out_specs=pl.BlockSpec((t, HEAD_BLOCK), lambda b, h, i: (b * nq + i, h)),
        out_shape=jax.ShapeDtypeStruct((n, N_HEADS * HEAD_BLOCK), BF16),
        scratch_shapes=[
            pltpu.VMEM((1, 2 * t), F32),
            pltpu.VMEM((1, 2 * t), F32),
            pltpu.VMEM((HEAD_BLOCK, 2 * t), F32),
        ],
        compiler_params=_params("arbitrary", "arbitrary", "arbitrary"),
        name="diff_attn",
    )(q, k, vt, bias, lam, sub_col)


def _out_proj_kernel(x_ref, o_ref, w_ref, g_ref, y_ref):
    y_ref[...] = x_ref[...] + g_ref[0] * _dot(o_ref[...], w_ref[...])


def _out_proj(x2, o, w, gate, seq):
    n, d = x2.shape
    tm = ROW_TILE
    tiles_per_batch = seq // tm
    return pl.pallas_call(
        _out_proj_kernel,
        grid=(n // tm,),
        in_specs=[
            pl.BlockSpec((tm, d), lambda i: (i, 0)),
            pl.BlockSpec((tm, o.shape[1]), lambda i: (i, 0)),
            pl.BlockSpec(w.shape, lambda i: (0, 0)),
            pl.BlockSpec((1, 1, d), lambda i: (i // tiles_per_batch, 0, 0)),
        ],
        out_specs=pl.BlockSpec((tm, d), lambda i: (i, 0)),
        out_shape=jax.ShapeDtypeStruct((n, d), F32),
        compiler_params=_params("arbitrary"),
        name="out_proj",
    )(x2, o, w, gate)


def _top_rows(s, count, val_ref, idx_ref, extra=()):
    rows = s.shape[0]
    iota = lax.broadcasted_iota(jnp.int32, s.shape, 0)
    for r in range(count):
        m = jnp.max(s, axis=0, keepdims=True)
        idx = jnp.min(jnp.where(s == m, iota, rows), axis=0, keepdims=True)
        hit = iota == idx
        val_ref[pl.ds(r, 1), :] = m
        if idx_ref is not None:
            idx_ref[pl.ds(r, 1), :] = idx
        for table, ref in extra:
            ref[pl.ds(r, 1), :] = jnp.max(jnp.where(hit, table, -1), axis=0, keepdims=True)
        s = jnp.where(hit, -jnp.inf, s)


def _peer_route_kernel(x_ref, gain_ref, shift_ref, scale_ref, wqt_ref, sk_ref,
                       ia_ref, ib_ref, gate_ref,
                       qt_ref, v1_ref, i1_ref, v2_ref, i2_ref, tv_ref, ta_ref, tb_ref):
    h = _norm_mod(x_ref[...], gain_ref[...], shift_ref[0], scale_ref[0]).astype(BF16)
    qt_ref[...] = _nt_dot(wqt_ref[...], h)
    k = PEER_TOPK
    half = PEER_N_KEYS

    def head(hd, carry):
        base = pl.multiple_of(hd * 2 * half, 2 * half)
        s1 = _dot(sk_ref[0], qt_ref[pl.ds(base, half), :].astype(BF16))
        s2 = _dot(sk_ref[1], qt_ref[pl.ds(base + half, half), :].astype(BF16))
        _top_rows(s1, k, v1_ref, i1_ref)
        _top_rows(s2, k, v2_ref, i2_ref)
        v1, i1, v2, i2 = v1_ref[...], i1_ref[...], v2_ref[...], i2_ref[...]
        tok = v1.shape[1]
        spans = [(0, k)] + [(a, k // 2) for a in range(1, k // 2)]
        cand = [v1[a:a + 1] + v2[:nb] for a, nb in spans] + [v1[k // 2:] + v2[0:1]]
        cand_a = ([jnp.broadcast_to(i1[a:a + 1], (nb, tok)) for a, nb in spans] + [i1[k // 2:]])
        cand_b = [i2[:nb] for a, nb in spans] + [jnp.broadcast_to(i2[0:1], (k - k // 2, tok))]
        _top_rows(jnp.concatenate(cand, axis=0), k, tv_ref, None,
                  extra=((jnp.concatenate(cand_a, axis=0), ta_ref), (jnp.concatenate(cand_b, axis=0), tb_ref)))
        top = tv_ref[...]
        e = jnp.exp(top - jnp.max(top, axis=0, keepdims=True))
        gate = e / jnp.sum(e, axis=0, keepdims=True)
        row = pl.multiple_of(hd * k, k)
        ia_ref[pl.ds(row, k), :] = ta_ref[...]
        ib_ref[pl.ds(row, k), :] = tb_ref[...]
        gate_ref[pl.ds(row, k), :] = gate
        return carry

    lax.fori_loop(0, PEER_HEADS, head, 0)


def _peer_route(x2, gain, shift, scale, wqt, sk, seq):
    n, d = x2.shape
    t = ROUTE_TILE
    tiles_per_batch = seq // t
    const2 = lambda i: (0, 0)
    mod_spec = pl.BlockSpec((1, 1, d), lambda i: (i // tiles_per_batch, 0, 0))
    slot_spec = pl.BlockSpec((PEER_SLOTS, t), lambda i: (0, i))
    k = PEER_TOPK
    return pl.pallas_call(
        _peer_route_kernel,
        grid=(n // t,),
        in_specs=[
            pl.BlockSpec((t, d), lambda i: (i, 0)),
            pl.BlockSpec((1, d), const2),
            mod_spec, mod_spec,
            pl.BlockSpec(wqt.shape, const2),
            pl.BlockSpec(sk.shape, lambda i: (0, 0, 0)),
        ],
        out_specs=[slot_spec, slot_spec, slot_spec],
        out_shape=[
            jax.ShapeDtypeStruct((PEER_SLOTS, n), jnp.int32),
            jax.ShapeDtypeStruct((PEER_SLOTS, n), jnp.int32),
            jax.ShapeDtypeStruct((PEER_SLOTS, n), F32),
        ],
        scratch_shapes=[
            pltpu.VMEM((wqt.shape[0], t), F32),
            pltpu.VMEM((k, t), F32), pltpu.VMEM((k, t), jnp.int32),
            pltpu.VMEM((k, t), F32), pltpu.VMEM((k, t), jnp.int32),
            pltpu.VMEM((k, t), F32), pltpu.VMEM((k, t), jnp.int32), pltpu.VMEM((k, t), jnp.int32),
        ],
        compiler_params=_params("arbitrary"),
        name="peer_route",
    )(x2, gain, shift, scale, wqt, sk)


def _peer_dense_kernel(x_ref, gain_ref, shift_ref, scale_ref, gres_ref, ia_ref, ib_ref, gate_ref,
                       u_ref, v_ref, y_ref, w_ref, h_ref, acc_ref, iat_ref, ibt_ref, gt_ref):
    j = pl.program_id(1)
    t = x_ref.shape[0]
    nk = PEER_N_KEYS

    @pl.when(j == 0)
    def _():
        h_ref[...] = _norm_mod(x_ref[...], gain_ref[...], shift_ref[0], scale_ref[0]).astype(BF16)
        acc_ref[...] = jnp.zeros(acc_ref.shape, F32)
        iat_ref[...] = ia_ref[...].T
        ibt_ref[...] = ib_ref[...].T
        gt_ref[...] = gate_ref[...].T
        key = lax.broadcasted_iota(jnp.int32, (nk, PEER_SLOTS), 0)
        sub = 8

        def build(grp, carry):
            base = pl.multiple_of(grp * PEER_BUILD_GROUP, PEER_BUILD_GROUP)
            slabs = []
            for s0 in range(0, PEER_BUILD_GROUP, sub):
                grids = []
                for i in range(sub):
                    row = pl.ds(base + s0 + i, 1)
                    a_hot = jnp.where(key == iat_ref[row, :], 1.0, 0.0).astype(BF16)
                    b_gate = jnp.where(key == ibt_ref[row, :], gt_ref[row, :], 0.0).astype(BF16)
                    grids.append(_nt_dot(a_hot, b_gate))
                slabs.append(pltpu.einshape("tab->atb", jnp.stack(grids, axis=0)))
            w_ref[:, pl.ds(base, PEER_BUILD_GROUP), :] = jnp.concatenate(slabs, axis=1).astype(BF16)
            return carry

        lax.fori_loop(0, t // PEER_BUILD_GROUP, build, 0)

    act = _nt_dot(h_ref[...], u_ref[...])
    parts = []
    for al in range(PEER_A_CHUNK):
        w_a = w_ref[j * PEER_A_CHUNK + al]
        parts.append(w_a * jax.nn.gelu(act[:, al * nk:(al + 1) * nk]).astype(BF16))
    acc_ref[...] += _dot(jnp.concatenate(parts, axis=1), v_ref[...])

    @pl.when(j == pl.num_programs(1) - 1)
    def _():
        y_ref[...] = x_ref[...] + gres_ref[0] * acc_ref[...]


def _peer_dense(x2, gain, shift, scale, gres, ia, ib, gate, u, v, seq):
    n, d = x2.shape
    t = PEER_TILE
    tiles_per_batch = seq // t
    ec = PEER_A_CHUNK * PEER_N_KEYS
    n_exp = u.shape[0]
    mod_spec = pl.BlockSpec((1, 1, d), lambda i, j: (i // tiles_per_batch, 0, 0))
    slot_spec = pl.BlockSpec((PEER_SLOTS, t), lambda i, j: (0, i))
    return pl.pallas_call(
        _peer_dense_kernel,
        grid=(n // t, n_exp // ec),
        in_specs=[
            pl.BlockSpec((t, d), lambda i, j: (i, 0)),
            pl.BlockSpec((1, d), lambda i, j: (0, 0)),
            mod_spec, mod_spec, mod_spec,
            slot_spec, slot_spec, slot_spec,
            pl.BlockSpec((ec, d), lambda i, j: (j, 0)),
            pl.BlockSpec((ec, d), lambda i, j: (j, 0)),
        ],
        out_specs=pl.BlockSpec((t, d), lambda i, j: (i, 0)),
        out_shape=jax.ShapeDtypeStruct((n, d), F32),
        scratch_shapes=[
            pltpu.VMEM((PEER_N_KEYS, t, PEER_N_KEYS), BF16),
            pltpu.VMEM((t, d), BF16),
            pltpu.VMEM((t, d), F32),
            pltpu.VMEM((t, PEER_SLOTS), jnp.int32),
            pltpu.VMEM((t, PEER_SLOTS), jnp.int32),
            pltpu.VMEM((t, PEER_SLOTS), F32),
        ],
        compiler_params=_params("arbitrary", "arbitrary"),
        name="peer_dense",
    )(x2, gain, shift, scale, gres, ia, ib, gate, u, v)


def _peer_layer(x2, gain, shift, scale, gres, wqt, sk, u, v, seq):
    ia, ib, gate = _peer_route(x2, gain, shift, scale, wqt, sk, seq)
    return _peer_dense(x2, gain, shift, scale, gres, ia, ib, gate, u, v, seq)


def _kv_proj_kernel(x_ref, gain_ref, shift_ref, scale_ref, wn_ref, wvt_ref, kg_ref,
                    cmp_ref, k_ref, vt_ref):
    h = _norm_mod(x_ref[...], gain_ref[...], shift_ref[0], scale_ref[0]).astype(BF16)
    kvn = _dot(h, wn_ref[...])
    hb = HEAD_BLOCK
    for c in range(4):
        cmp_ref[c] = kvn[:, c * hb:(c + 1) * hb].astype(BF16)
    for c in range(4):
        kk = kvn[:, (4 + c) * hb:(5 + c) * hb]
        kn = kk * lax.rsqrt(jnp.mean(kk * kk, axis=-1, keepdims=True) + EPS) * kg_ref[pl.ds(c // 2, 1), :]
        k_ref[c] = kn.astype(BF16)
    vt = _nt_dot(wvt_ref[...], h)
    for c in range(4):
        for j in range(vt.shape[1] // ATT_TILE):
            vt_ref[c, j] = vt[c * hb:(c + 1) * hb, j * ATT_TILE:(j + 1) * ATT_TILE].astype(BF16)


def _kv_proj(x2, gain, shift, scale, wn, wvt, kg, seq):
    n, d = x2.shape
    tm = ROW_TILE
    tiles_per_batch = seq // tm
    const2 = lambda i: (0, 0)
    mod_spec = pl.BlockSpec((1, 1, d), lambda i: (i // tiles_per_batch, 0, 0))
    return pl.pallas_call(
        _kv_proj_kernel,
        grid=(n // tm,),
        in_specs=[
            pl.BlockSpec((tm, d), lambda i: (i, 0)),
            pl.BlockSpec((1, d), const2),
            mod_spec, mod_spec,
            pl.BlockSpec(wn.shape, const2),
            pl.BlockSpec(wvt.shape, const2),
            pl.BlockSpec(kg.shape, const2),
        ],
        out_specs=[
            pl.BlockSpec((4, tm, HEAD_BLOCK), lambda i: (0, i, 0)),
            pl.BlockSpec((4, tm, HEAD_BLOCK), lambda i: (0, i, 0)),
            pl.BlockSpec((4, tm // ATT_TILE, HEAD_BLOCK, ATT_TILE), lambda i: (0, i, 0, 0)),
        ],
        out_shape=[
            jax.ShapeDtypeStruct((4, n, HEAD_BLOCK), BF16),
            jax.ShapeDtypeStruct((4, n, HEAD_BLOCK), BF16),
            jax.ShapeDtypeStruct((4, n // ATT_TILE, HEAD_BLOCK, ATT_TILE), BF16),
        ],
        compiler_params=_params("arbitrary"),
        name="kv_proj",
    )(x2, gain, shift, scale, wn, wvt, kg)


def _compress_kernel(xk_ref, xv_ref, w1_ref, pos_ref, b1_ref, w2_ref, kg_ref, kc_ref, vct_ref):
    half = w1_ref.shape[1] // 2
    rows = xk_ref.shape[2]
    for kv, x_ref in ((0, xk_ref), (1, xv_ref)):
        x = x_ref[0, 0]
        w1 = w1_ref[kv]
        first = _dot(x, w1[:half])
        second = _dot(x, w1[half:])
        hid = first + pltpu.roll(second, rows - 1, axis=0)
        pos_term = _dot(jnp.broadcast_to(pos_ref[kv], (8, 2 * half)).astype(BF16), w1)[0:1]
        hid = jax.nn.gelu(hid + pos_term + b1_ref[kv])
        out = _dot(hid.astype(BF16), w2_ref[kv])
        if kv == 0:
            out = out * lax.rsqrt(jnp.mean(out * out, axis=-1, keepdims=True) + EPS) * kg_ref[...]
            kc_ref[0, 0] = out.astype(BF16)
        else:
            vct_ref[0, 0] = out.T.astype(BF16)


def _compress(xr, w1, pos, b1, w2, kg, batch):
    _, _, rows, width = xr.shape
    g = NSA_KV_GROUPS
    return pl.pallas_call(
        _compress_kernel,
        grid=(batch, g),
        in_specs=[
            pl.BlockSpec((1, 1, rows, width), lambda b, gi: (gi, b, 0, 0)),
            pl.BlockSpec((1, 1, rows, width), lambda b, gi: (g + gi, b, 0, 0)),
            pl.BlockSpec(w1.shape, lambda b, gi: (0, 0, 0)),
            pl.BlockSpec(pos.shape, lambda b, gi: (0, 0, 0)),
            pl.BlockSpec(b1.shape, lambda b, gi: (0, 0, 0)),
            pl.BlockSpec(w2.shape, lambda b, gi: (0, 0, 0)),
            pl.BlockSpec(kg.shape, lambda b, gi: (0, 0)),
        ],
        out_specs=[
            pl.BlockSpec((1, 1, rows, HEAD_BLOCK), lambda b, gi: (b, gi, 0, 0)),
            pl.BlockSpec((1, 1, HEAD_BLOCK, rows), lambda b, gi: (b, gi, 0, 0)),
        ],
        out_shape=[
            jax.ShapeDtypeStruct((batch, g, rows, HEAD_BLOCK), BF16),
            jax.ShapeDtypeStruct((batch, g, HEAD_BLOCK, rows), BF16),
        ],
        compiler_params=_params("arbitrary", "arbitrary"),
        name="compress",
    )(xr, xr, w1, pos, b1, w2, kg)


def _nsa_q_proj_kernel(x_ref, gain_ref, shift_ref, scale_ref, wq_ref, wgt_ref, bg_ref, qg_ref,
                       q_ref, gt_ref):
    h = _norm_mod(x_ref[...], gain_ref[...], shift_ref[0], scale_ref[0]).astype(BF16)
    q = _dot(h, wq_ref[...])
    hb = HEAD_BLOCK
    for c in range(N_HEADS):
        qq = q[:, c * hb:(c + 1) * hb]
        qn = qq * lax.rsqrt(jnp.mean(qq * qq, axis=-1, keepdims=True) + EPS) * qg_ref[...]
        q_ref[c] = qn.astype(BF16)
    gt_ref[...] = jax.nn.sigmoid(_nt_dot(wgt_ref[...], h) + bg_ref[...])


def _nsa_q_proj(x2, gain, shift, scale, wq, wgt, bg, qg, seq):
    n, d = x2.shape
    tm = ROW_TILE
    tiles_per_batch = seq // tm
    const2 = lambda i: (0, 0)
    mod_spec = pl.BlockSpec((1, 1, d), lambda i: (i // tiles_per_batch, 0, 0))
    rows = wgt.shape[0]
    return pl.pallas_call(
        _nsa_q_proj_kernel,
        grid=(n // tm,),
        in_specs=[
            pl.BlockSpec((tm, d), lambda i: (i, 0)),
            pl.BlockSpec((1, d), const2),
            mod_spec, mod_spec,
            pl.BlockSpec(wq.shape, const2),
            pl.BlockSpec(wgt.shape, const2),
            pl.BlockSpec(bg.shape, const2),
            pl.BlockSpec(qg.shape, const2),
        ],
        out_specs=[
            pl.BlockSpec((N_HEADS, tm, HEAD_BLOCK), lambda i: (0, i, 0)),
            pl.BlockSpec((rows, tm), lambda i: (0, i)),
        ],
        out_shape=[
            jax.ShapeDtypeStruct((N_HEADS, n, HEAD_BLOCK), BF16),
            jax.ShapeDtypeStruct((rows, n), F32),
        ],
        compiler_params=_params("arbitrary"),
        name="nsa_q_proj",
    )(x2, gain, shift, scale, wq, wgt, bg, qg)


def _nsa_attn_kernel(n_slc_bias, n_win_bias, top_n,
                     q_ref, g_ref, kc_ref, vct_ref, ks_ref, vst_ref, kw_ref, vwt_ref,
                     cb_ref, sb_ref, wb_ref, ov_ref, o_ref,
                     sel_ref, oc_ref, m_ref, l_ref, acc_ref, os_ref):
    qi = pl.program_id(2)
    nq = pl.num_programs(2)
    t = ATT_TILE
    hg = NSA_HEADS_PER_GROUP
    q = q_ref[...].reshape(hg * t, HEAD_BLOCK)
    n_cmp_rows = kc_ref.shape[2]
    n_blk = ov_ref.shape[0]

    off = pl.multiple_of((nq - 1 - qi) * (t // CMP_STRIDE), t // CMP_STRIDE)
    bias_c = cb_ref[0, pl.ds(off, n_cmp_rows), :]
    sc = _nt_dot(kc_ref[0, 0], q) + bias_c
    visible = bias_c > 0.5 * NEG_INF
    e = jnp.where(visible, jnp.exp(sc - jnp.max(sc, axis=0, keepdims=True)), 0.0)
    lsum = jnp.sum(e, axis=0, keepdims=True)
    p = e / jnp.where(lsum > 0.0, lsum, 1.0)
    oc_ref[...] = _dot(vct_ref[0, 0], p.astype(BF16))

    psum = p[:, 0:t]
    for i in range(1, hg):
        psum = psum + p[:, i * t:(i + 1) * t]
    p_hi = psum.astype(BF16)
    p_lo = (psum - p_hi.astype(F32)).astype(BF16)
    imp = _dot(ov_ref[...], p_hi) + _dot(ov_ref[...], p_lo)
    blk = lax.broadcasted_iota(jnp.int32, imp.shape, 0)
    pos = qi * t + lax.broadcasted_iota(jnp.int32, imp.shape, 1)
    back = pos // SEL_BLOCK - blk
    forced = (blk == 0) | ((back >= 0) & (back < SEL_N_LOCAL))
    imp = jnp.where(forced, FORCE, imp)
    imp = jnp.where(back >= 0, imp, NEG_INF)
    sel = jnp.zeros(imp.shape, F32)
    for _ in range(top_n):
        m = jnp.max(imp, axis=0, keepdims=True)
        idx = jnp.min(jnp.where(imp == m, blk, n_blk), axis=0, keepdims=True)
        hit = blk == idx
        sel = jnp.where(hit & (m >= 0.0), 1.0, sel)
        imp = jnp.where(hit, -jnp.inf, imp)
    sel_ref[...] = jnp.where(sel > 0.0, 0.0, NEG_INF)

    blocks_per_tile = t // SEL_BLOCK

    _softmax_init(m_ref, l_ref, acc_ref)

    def slc_body(kt, carry):
        k = ks_ref[0, pl.ds(pl.multiple_of(kt * t, t), t), :]
        bias = sb_ref[0, jnp.minimum(qi - kt, n_slc_bias - 1)]
        rows = [jnp.broadcast_to(sel_ref[pl.ds(kt * blocks_per_tile + jj, 1), :], (SEL_BLOCK, t))
                for jj in range(blocks_per_tile)]
        mask = jnp.concatenate(rows, axis=0)
        s = _nt_dot(k, q) + bias + jnp.concatenate([mask] * hg, axis=1)
        _online_softmax_step(s, vst_ref[0, kt], m_ref, l_ref, acc_ref)
        return carry

    lax.fori_loop(0, qi + 1, slc_body, 0)
    os_ref[...] = acc_ref[...] / l_ref[...]

    _softmax_init(m_ref, l_ref, acc_ref)

    def win_body(kt, carry):
        k = kw_ref[0, pl.ds(pl.multiple_of(kt * t, t), t), :]
        s = _nt_dot(k, q) + wb_ref[0, qi - kt]
        _online_softmax_step(s, vwt_ref[0, kt], m_ref, l_ref, acc_ref)
        return carry

    lax.fori_loop(jnp.maximum(qi - (n_win_bias - 1), 0), qi + 1, win_body, 0)
    ow = acc_ref[...] / l_ref[...]

    gates = g_ref[...]
    for i in range(hg):
        sl = slice(i * t, (i + 1) * t)
        o = (gates[3 * i:3 * i + 1] * oc_ref[:, sl] + gates[3 * i + 1:3 * i + 2] * os_ref[:, sl]
             + gates[3 * i + 2:3 * i + 3] * ow[:, sl])
        o_ref[:, i * HEAD_BLOCK:(i + 1) * HEAD_BLOCK] = o.T.astype(BF16)


def _nsa_attn(q, gt, kc, vct, k, vt, cmp_bias, slc_bias, win_bias, ovt, batch, seq):
    t = ATT_TILE
    nq = seq // t
    hg = NSA_HEADS_PER_GROUP
    g = NSA_KV_GROUPS
    n = batch * seq
    top_n = min(SEL_TOPN, seq // SEL_BLOCK)
    lanes = hg * t
    idx3 = lambda b, gi, i: (gi, 0, 0)
    idx4 = lambda b, gi, i: (gi, 0, 0, 0)
    return pl.pallas_call(
        functools.partial(_nsa_attn_kernel, slc_bias.shape[1], win_bias.shape[1], top_n),
        grid=(batch, g, nq),
        in_specs=[
            pl.BlockSpec((hg, t, HEAD_BLOCK), lambda b, gi, i: (gi, b * nq + i, 0)),
            pl.BlockSpec((GATE_ROWS, t), lambda b, gi, i: (gi, b * nq + i)),
            pl.BlockSpec((1, 1) + kc.shape[2:], lambda b, gi, i: (b, gi, 0, 0)),
            pl.BlockSpec((1, 1) + vct.shape[2:], lambda b, gi, i: (b, gi, 0, 0)),
            pl.BlockSpec((1, seq, HEAD_BLOCK), lambda b, gi, i: (gi, b, 0)),
            pl.BlockSpec((1, nq, HEAD_BLOCK, t), lambda b, gi, i: (gi, b, 0, 0)),
            pl.BlockSpec((1, seq, HEAD_BLOCK), lambda b, gi, i: (g + gi, b, 0)),
            pl.BlockSpec((1, nq, HEAD_BLOCK, t), lambda b, gi, i: (g + gi, b, 0, 0)),
            pl.BlockSpec((1,) + cmp_bias.shape[1:], idx3),
            pl.BlockSpec((1,) + slc_bias.shape[1:], idx4),
            pl.BlockSpec((1,) + win_bias.shape[1:], idx4),
            pl.BlockSpec(ovt.shape, lambda b, gi, i: (0, 0)),
        ],
        out_specs=pl.BlockSpec((t, hg * HEAD_BLOCK), lambda b, gi, i: (b * nq + i, gi)),
        out_shape=jax.ShapeDtypeStruct((n, N_HEADS * HEAD_BLOCK), BF16),
        scratch_shapes=[
            pltpu.VMEM((ovt.shape[0], t), F32),
            pltpu.VMEM((HEAD_BLOCK, lanes), F32),
            pltpu.VMEM((1, lanes), F32),
            pltpu.VMEM((1, lanes), F32),
            pltpu.VMEM((HEAD_BLOCK, lanes), F32),
            pltpu.VMEM((HEAD_BLOCK, lanes), F32),
        ],
        compiler_params=_params("arbitrary", "arbitrary", "arbitrary"),
        name="nsa_attn",
    )(q, gt, kc, vct, k, vt, k, vt, cmp_bias, slc_bias, win_bias, ovt)


def _skew_rows(v, rows):
    heads, length = v.shape
    ext = jnp.concatenate([v, jnp.zeros((heads, 1), v.dtype)], axis=1)
    return jnp.tile(ext, (1, rows))[:, :rows * length].reshape(heads, rows, length)


def _bias_tiles(profile, t, n_tiles, window=None):
    heads = profile.shape[0]
    span = n_tiles * t
    vis = profile[:, :span]
    if window is not None:
        vis = jnp.where(jnp.arange(span)[None] < window, vis, NEG_INF)
    v = jnp.concatenate([jnp.full((heads, t), NEG_INF, F32), vis], axis=1)
    m = _skew_rows(v, t)[:, :, t:]
    return m.reshape(heads, t, n_tiles, t).transpose(0, 2, 1, 3)


def _cmp_bias_strip(profile, seq):
    t = ATT_TILE
    heads = profile.shape[0]
    per = t // CMP_STRIDE
    n_rho = 2 * (seq // t) - 1
    c = seq - t - (CMP_LEN - 1)
    pad = t * (n_rho - 1) + CMP_STRIDE * (per - 1) - c
    g = jnp.concatenate([jnp.full((heads, pad), NEG_INF, F32), profile], axis=1)
    slabs = []
    for sigma in range(per):
        base = c - CMP_STRIDE * sigma + pad
        seg = g[:, base - t * (n_rho - 1):base + t].reshape(heads, n_rho, t)
        slabs.append(seg[:, ::-1])
    return jnp.stack(slabs, axis=2).reshape(heads, n_rho * per, t)


def _n_far_tiles(t):
    return -(-(REL_MAX_DIST + t - 1) // t) + 1


def _group_lanes(tiles):
    hg = NSA_HEADS_PER_GROUP
    parts = tiles.reshape((NSA_KV_GROUPS, hg) + tiles.shape[1:])
    return jnp.concatenate([parts[:, i] for i in range(hg)], axis=-1)


def kernel(x, c, rel_bias, ada_w, ada_b, norm_mix, norm_ffn, diff_w_in, diff_w_out, diff_lambda,
           diff_q_gain, diff_k_gain, diff_subln, kv_norm, kv_ada_w, kv_ada_b, kv_w, cmp_pos, cmp_w1,
           cmp_b1, cmp_w2, nsa_k_gain, nsa_w_in, nsa_b_gate, nsa_w_out, nsa_q_gain, peer_w_q,
           peer_subkeys, peer_u, peer_v):
    batch, seq, d = x.shape
    n = batch * seq
    t = ATT_TILE
    nq = seq // t
    width = N_HEADS * HEAD_BLOCK
    x2 = x.reshape(n, d)

    c_pad = jnp.pad(c, ((0, 8 - batch % 8 if batch % 8 else 0), (0, 0)))
    mods = _ada(c_pad, ada_w, ada_b, 1536)
    kv_mods = _ada(c_pad, kv_ada_w[None], kv_ada_b[None], 1024)

    def mod(arr, layer, k):
        return arr[layer, :batch, k * d:(k + 1) * d].reshape(batch, 1, d)

    profile = rel_bias[_rel_bucket(jnp.arange(seq))].T

    lam_init = 0.8 - 0.6 * math.exp(-0.3 * 0)
    w_in = diff_w_in[0]
    wqk = w_in[:, :2 * width].astype(BF16)
    wvt = w_in[:, 2 * width:].T.astype(BF16)
    grp = jnp.asarray(np.kron(np.eye(width // DIFF_HEAD_DIM), np.ones((DIFF_HEAD_DIM, DIFF_HEAD_DIM))), BF16)
    reps = width // DIFF_HEAD_DIM
    qg = (jnp.tile(diff_q_gain[0], reps) * DIFF_HEAD_DIM ** -0.5).reshape(1, width)
    kg = jnp.tile(diff_k_gain[0], reps).reshape(1, width)
    q, k, vt = _diff_proj(x2, norm_mix[0:1], mod(mods, 0, 0), mod(mods, 0, 1), wqk, wvt, grp, qg, kg, seq)
    diff_bias = _bias_tiles(profile, DIFF_TILE, min(_n_far_tiles(DIFF_TILE), seq // DIFF_TILE))
    sub_col = (diff_subln[0] * (1.0 - lam_init)).reshape(HEAD_BLOCK, 1)
    o = _diff_attn(q, k, vt, diff_bias, diff_lambda[0], sub_col, batch, seq, lam_init)
    x2 = _out_proj(x2, o, diff_w_out[0].astype(BF16), mod(mods, 0, 2), seq)

    def peer(x2, layer):
        return _peer_layer(x2, norm_ffn[layer:layer + 1], mod(mods, layer, 3), mod(mods, layer, 4),
                           mod(mods, layer, 5), peer_w_q[layer].T.astype(BF16),
                           peer_subkeys[layer].astype(BF16), peer_u[layer].astype(BF16),
                           peer_v[layer].astype(BF16), seq)

    x2 = peer(x2, 0)

    hb = HEAD_BLOCK

    def kv_cols(branch, kv, g):
        start = branch * 4 * hb + kv * 2 * hb + g * hb
        return kv_w[:, start:start + hb]

    wn = jnp.concatenate([kv_cols(0, 0, 0), kv_cols(0, 0, 1), kv_cols(0, 1, 0), kv_cols(0, 1, 1),
                          kv_cols(1, 0, 0), kv_cols(1, 0, 1), kv_cols(2, 0, 0), kv_cols(2, 0, 1)],
                         axis=1).astype(BF16)
    wvt_kv = jnp.concatenate([kv_cols(1, 1, 0), kv_cols(1, 1, 1), kv_cols(2, 1, 0), kv_cols(2, 1, 1)],
                             axis=1).T.astype(BF16)
    cmp_raw, k_sw, vt_sw = _kv_proj(x2, kv_norm.reshape(1, d), mod(kv_mods, 0, 0), mod(kv_mods, 0, 1),
                                    wn, wvt_kv, nsa_k_gain[1:3], seq)
    xr = cmp_raw.reshape(4, batch, seq // CMP_STRIDE, CMP_STRIDE * hb)
    kc, vct = _compress(xr, cmp_w1.astype(BF16), cmp_pos.reshape(2, 1, CMP_LEN * hb),
                        cmp_b1.reshape(2, 1, CMP_HIDDEN), cmp_w2.astype(BF16), nsa_k_gain[0:1], batch)

    hg = NSA_HEADS_PER_GROUP
    w_in1 = nsa_w_in[0]
    wq = w_in1[:, :width].astype(BF16)
    w_gate = w_in1[:, width:].reshape(d, NSA_KV_GROUPS, hg * N_BRANCH)
    w_gate = jnp.pad(w_gate, ((0, 0), (0, 0), (0, GATE_ROWS - hg * N_BRANCH)))
    wgt = w_gate.reshape(d, NSA_KV_GROUPS * GATE_ROWS).T.astype(BF16)
    b_gate = jnp.pad(nsa_b_gate[0].reshape(NSA_KV_GROUPS, hg * N_BRANCH),
                     ((0, 0), (0, GATE_ROWS - hg * N_BRANCH))).reshape(NSA_KV_GROUPS * GATE_ROWS, 1)
    qg1 = (nsa_q_gain[0] * hb ** -0.5).reshape(1, hb)
    q1, gt = _nsa_q_proj(x2, norm_mix[1:2], mod(mods, 1, 0), mod(mods, 1, 1), wq, wgt, b_gate, qg1, seq)

    slc_bias = _group_lanes(_bias_tiles(profile, t, min(_n_far_tiles(t), nq)))
    win_bias = _group_lanes(_bias_tiles(profile, t, min(WINDOW // t + 1, nq), window=WINDOW))
    cmp_bias = _group_lanes(_cmp_bias_strip(profile, seq))
    n_rows = seq // CMP_STRIDE
    n_sel = seq // SEL_BLOCK
    cmp_start = np.arange(n_rows) * CMP_STRIDE
    sel_start = np.arange(n_sel) * SEL_BLOCK
    overlap = np.clip(np.minimum(cmp_start[:, None] + CMP_LEN, sel_start[None, :] + SEL_BLOCK)
                      - np.maximum(cmp_start[:, None], sel_start[None, :]), 0, None) / CMP_LEN
    overlap[(seq - CMP_LEN) // CMP_STRIDE + 1:] = 0.0
    ovt = jnp.asarray(overlap.T, BF16)
    o1 = _nsa_attn(q1, gt, kc, vct, k_sw, vt_sw, cmp_bias, slc_bias, win_bias, ovt, batch, seq)
    x2 = _out_proj(x2, o1, nsa_w_out[0].astype(BF16), mod(mods, 1, 2), seq)

    x2 = peer(x2, 1)
    return x2.reshape(batch, seq, d)
```

```python
import functools
import math

import jax
import jax.numpy as jnp
import numpy as np
from jax import lax
from jax.experimental import pallas as pl
from jax.experimental.pallas import tpu as pltpu

F32 = jnp.float32
BF16 = jnp.bfloat16

N_HEADS = 8
N_ADA = 6
EPS = 1e-6
NEG_INF = -1e30
FORCE = 1e9
LOG2_E = math.log2(math.e)

DIFF_HEAD_DIM = 64
HEAD_BLOCK = 128

NSA_KV_GROUPS = 2
NSA_HEADS_PER_GROUP = N_HEADS // NSA_KV_GROUPS
N_BRANCH = 3
CMP_LEN = 32
CMP_STRIDE = 16
CMP_HIDDEN = 256
SEL_BLOCK = 64
SEL_TOPN = 16
SEL_N_LOCAL = 2
WINDOW = 512
GATE_ROWS = 16

REL_BUCKETS = 32
REL_MAX_DIST = 1024

PEER_HEADS = 8
PEER_N_KEYS = 128
PEER_TOPK = 16
PEER_SLOTS = PEER_HEADS * PEER_TOPK

ROW_TILE = 512
ATT_TILE = 256
DIFF_TILE = 512
ROUTE_TILE = 256
PEER_TILE = 512
PEER_BUILD_GROUP = 16
PEER_A_CHUNK = 8
VMEM_LIMIT = 56 * 1024 * 1024

NT_DIMS = (((1,), (1,)), ((), ()))


def _params(*sem):
    return pltpu.CompilerParams(dimension_semantics=sem, vmem_limit_bytes=VMEM_LIMIT)


def _nt_dot(a, b):
    return lax.dot_general(a, b, NT_DIMS, preferred_element_type=F32)


def _dot(a, b):
    return jnp.dot(a, b, preferred_element_type=F32)


def _rel_bucket(dist):
    max_exact = REL_BUCKETS // 2
    n = jnp.maximum(dist, 0)
    nf = jnp.maximum(n, 1).astype(jnp.float32)
    large = max_exact + (jnp.log(nf / max_exact) / math.log(REL_MAX_DIST / max_exact)
                         * (REL_BUCKETS - max_exact)).astype(jnp.int32)
    large = jnp.minimum(large, REL_BUCKETS - 1)
    return jnp.where(n < max_exact, n, large)


def _norm_mod(x, gain, shift, scale):
    y = x * lax.rsqrt(jnp.mean(x * x, axis=-1, keepdims=True) + EPS) * gain
    return y * (1.0 + scale) + shift


def _ada_kernel(c_ref, w_ref, b_ref, o_ref):
    c = c_ref[...]
    ca = c * jax.nn.sigmoid(c)
    o_ref[0] = jnp.dot(ca, w_ref[0], preferred_element_type=F32,
                       precision=lax.Precision.HIGHEST) + b_ref[0]


def _ada(c_pad, w, b, col_tile):
    layers, d, n = w.shape
    rows = c_pad.shape[0]
    return pl.pallas_call(
        _ada_kernel,
        grid=(layers, n // col_tile),
        in_specs=[
            pl.BlockSpec((rows, d), lambda l, j: (0, 0)),
            pl.BlockSpec((1, d, col_tile), lambda l, j: (l, 0, j)),
            pl.BlockSpec((1, 1, col_tile), lambda l, j: (l, 0, j)),
        ],
        out_specs=pl.BlockSpec((1, rows, col_tile), lambda l, j: (l, 0, j)),
        out_shape=jax.ShapeDtypeStruct((layers, rows, n), F32),
        compiler_params=_params("arbitrary", "arbitrary"),
        name="ada",
    )(c_pad, w, b.reshape(layers, 1, n))


def _diff_proj_kernel(x_ref, gain_ref, shift_ref, scale_ref, wqk_ref, wvt_ref, grp_ref,
                      qg_ref, kg_ref, q_ref, k_ref, vt_ref):
    h = _norm_mod(x_ref[...], gain_ref[...], shift_ref[0], scale_ref[0]).astype(BF16)
    qk = _dot(h, wqk_ref[...])
    width = N_HEADS * HEAD_BLOCK
    for part, out_ref, g_ref in ((0, q_ref, qg_ref), (1, k_ref, kg_ref)):
        z = qk[:, part * width:(part + 1) * width]
        zz = z * z
        hi = zz.astype(BF16)
        lo = (zz - hi.astype(F32)).astype(BF16)
        ss = _dot(hi, grp_ref[...]) + _dot(lo, grp_ref[...])
        zn = z * lax.rsqrt(ss * (1.0 / DIFF_HEAD_DIM) + EPS) * g_ref[...]
        for c in range(N_HEADS):
            out_ref[c] = zn[:, c * HEAD_BLOCK:(c + 1) * HEAD_BLOCK].astype(BF16)
    vt = _nt_dot(wvt_ref[...], h)
    n_chunk = vt.shape[1] // DIFF_TILE
    for c in range(N_HEADS):
        for j in range(n_chunk):
            vt_ref[c, j] = vt[c * HEAD_BLOCK:(c + 1) * HEAD_BLOCK,
                              j * DIFF_TILE:(j + 1) * DIFF_TILE].astype(BF16)


def _diff_proj(x2, gain, shift, scale, wqk, wvt, grp, qg, kg, seq):
    n, d = x2.shape
    tm = ROW_TILE
    tiles_per_batch = seq // tm
    width = N_HEADS * HEAD_BLOCK
    const2 = lambda i: (0, 0)
    mod_spec = pl.BlockSpec((1, 1, d), lambda i: (i // tiles_per_batch, 0, 0))
    return pl.pallas_call(
        _diff_proj_kernel,
        grid=(n // tm,),
        in_specs=[
            pl.BlockSpec((tm, d), lambda i: (i, 0)),
            pl.BlockSpec((1, d), const2),
            mod_spec, mod_spec,
            pl.BlockSpec((d, 2 * width), const2),
            pl.BlockSpec((width, d), const2),
            pl.BlockSpec((width, width), const2),
            pl.BlockSpec((1, width), const2),
            pl.BlockSpec((1, width), const2),
        ],
        out_specs=[
            pl.BlockSpec((N_HEADS, tm, HEAD_BLOCK), lambda i: (0, i, 0)),
            pl.BlockSpec((N_HEADS, tm, HEAD_BLOCK), lambda i: (0, i, 0)),
            pl.BlockSpec((N_HEADS, tm // DIFF_TILE, HEAD_BLOCK, DIFF_TILE), lambda i: (0, i, 0, 0)),
        ],
        out_shape=[
            jax.ShapeDtypeStruct((N_HEADS, n, HEAD_BLOCK), BF16),
            jax.ShapeDtypeStruct((N_HEADS, n, HEAD_BLOCK), BF16),
            jax.ShapeDtypeStruct((N_HEADS, n // DIFF_TILE, HEAD_BLOCK, DIFF_TILE), BF16),
        ],
        compiler_params=_params("arbitrary"),
        name="diff_proj",
    )(x2, gain, shift, scale, wqk, wvt, grp, qg, kg)


def _online_softmax_step(s, vt, m_ref, l_ref, acc_ref):
    m_old = m_ref[...]
    m_new = jnp.maximum(m_old, jnp.max(s, axis=0, keepdims=True))
    alpha = jnp.exp2(m_old - m_new)
    p = jnp.exp2(s - m_new)
    l_ref[...] = alpha * l_ref[...] + jnp.sum(p, axis=0, keepdims=True)
    acc_ref[...] = alpha * acc_ref[...] + _dot(vt, p.astype(BF16))
    m_ref[...] = m_new


def _softmax_init(m_ref, l_ref, acc_ref):
    m_ref[...] = jnp.full(m_ref.shape, NEG_INF, F32)
    l_ref[...] = jnp.zeros(l_ref.shape, F32)
    acc_ref[...] = jnp.zeros(acc_ref.shape, F32)


def _diff_attn_kernel(lam_init, n_bias, q_ref, k_ref, vt_ref, bias_ref, lam_ref, sub_ref,
                      o_ref, m_ref, l_ref, acc_ref):
    qi = pl.program_id(2)
    t = DIFF_TILE
    q = q_ref[0]
    lane = lax.broadcasted_iota(jnp.int32, q.shape, 1)
    zero = jnp.zeros_like(q)
    qs = jnp.concatenate([jnp.where(lane < DIFF_HEAD_DIM, q, zero),
                          jnp.where(lane >= DIFF_HEAD_DIM, q, zero)], axis=0)
    _softmax_init(m_ref, l_ref, acc_ref)

    def body(kt, carry):
        k = k_ref[0, pl.ds(pl.multiple_of(kt * t, t), t), :]
        bias = bias_ref[0, jnp.minimum(qi - kt, n_bias - 1)]
        s = _nt_dot(k, qs) + jnp.concatenate([bias, bias], axis=1)
        _online_softmax_step(s, vt_ref[0, kt], m_ref, l_ref, acc_ref)
        return carry

    lax.fori_loop(0, qi + 1, body, 0)

    lam = lam_ref[...]
    lam_full = (jnp.exp(jnp.sum(lam[0:1] * lam[1:2], axis=1, keepdims=True))
                - jnp.exp(jnp.sum(lam[2:3] * lam[3:4], axis=1, keepdims=True)) + lam_init)
    o = acc_ref[...] / l_ref[...]
    o = o[:, :t] - lam_full * o[:, t:]
    y = o * lax.rsqrt(jnp.mean(o * o, axis=0, keepdims=True) + EPS) * sub_ref[...]
    o_ref[...] = y.T.astype(BF16)


def _diff_attn(q, k, vt, bias, lam, sub_col, batch, seq, lam_init):
    t = DIFF_TILE
    nq = seq // t
    n_bias = bias.shape[1]
    n = batch * seq
    return pl.pallas_call(
        functools.partial(_diff_attn_kernel, lam_init, n_bias),
        grid=(batch, N_HEADS, nq),
        in_specs=[
            pl.BlockSpec((1, t, HEAD_BLOCK), lambda b, h, i: (h, b * nq + i, 0)),
            pl.BlockSpec((1, seq, HEAD_BLOCK), lambda b, h, i: (h, b, 0)),
            pl.BlockSpec((1, nq, HEAD_BLOCK, t), lambda b, h, i: (h, b, 0, 0)),
            pl.BlockSpec((1, n_bias, t, t), lambda b, h, i: (h, 0, 0, 0)),
            pl.BlockSpec(lam.shape, lambda b, h, i: (0, 0)),
            pl.BlockSpec(sub_col.shape, lambda b, h, i: (0, 0)),
        ],
        out_specs=pl.BlockSpec((t, HEAD_BLOCK), lambda b, h, i: (b * nq + i, h)),
        out_shape=jax.ShapeDtypeStruct((n, N_HEADS * HEAD_BLOCK), BF16),
        scratch_shapes=[
            pltpu.VMEM((1, 2 * t), F32),
            pltpu.VMEM((1, 2 * t), F32),
            pltpu.VMEM((HEAD_BLOCK, 2 * t), F32),
        ],
        compiler_params=_params("arbitrary", "arbitrary", "arbitrary"),
        name="diff_attn",
    )(q, k, vt, bias, lam, sub_col)


def _out_proj_kernel(x_ref, o_ref, w_ref, g_ref, y_ref):
    y_ref[...] = x_ref[...] + g_ref[0] * _dot(o_ref[...], w_ref[...])


def _out_proj(x2, o, w, gate, seq):
    n, d = x2.shape
    tm = ROW_TILE
    tiles_per_batch = seq // tm
    return pl.pallas_call(
        _out_proj_kernel,
        grid=(n // tm,),
        in_specs=[
            pl.BlockSpec((tm, d), lambda i: (i, 0)),
            pl.BlockSpec((tm, o.shape[1]), lambda i: (i, 0)),
            pl.BlockSpec(w.shape, lambda i: (0, 0)),
            pl.BlockSpec((1, 1, d), lambda i: (i // tiles_per_batch, 0, 0)),
        ],
        out_specs=pl.BlockSpec((tm, d), lambda i: (i, 0)),
        out_shape=jax.ShapeDtypeStruct((n, d), F32),
        compiler_params=_params("arbitrary"),
        name="out_proj",
    )(x2, o, w, gate)


def _top_rows(s, count, val_ref, idx_ref, extra=()):
    rows = s.shape[0]
    iota = lax.broadcasted_iota(jnp.int32, s.shape, 0)
    for r in range(count):
        m = jnp.max(s, axis=0, keepdims=True)
        idx = jnp.min(jnp.where(s == m, iota, rows), axis=0, keepdims=True)
        hit = iota == idx
        val_ref[pl.ds(r, 1), :] = m
        if idx_ref is not None:
            idx_ref[pl.ds(r, 1), :] = idx
        for table, ref in extra:
            ref[pl.ds(r, 1), :] = jnp.max(jnp.where(hit, table, -1), axis=0, keepdims=True)
        s = jnp.where(hit, -jnp.inf, s)


def _peer_route_kernel(x_ref, gain_ref, shift_ref, scale_ref, wqt_ref, sk_ref,
                       ia_ref, ib_ref, gate_ref,
                       qt_ref, v1_ref, i1_ref, v2_ref, i2_ref, tv_ref, ta_ref, tb_ref):
    h = _norm_mod(x_ref[...], gain_ref[...], shift_ref[0], scale_ref[0]).astype(BF16)
    qt_ref[...] = _nt_dot(wqt_ref[...], h)
    k = PEER_TOPK
    half = PEER_N_KEYS

    def head(hd, carry):
        base = pl.multiple_of(hd * 2 * half, 2 * half)
        s1 = _dot(sk_ref[0], qt_ref[pl.ds(base, half), :].astype(BF16))
        s2 = _dot(sk_ref[1], qt_ref[pl.ds(base + half, half), :].astype(BF16))
        _top_rows(s1, k, v1_ref, i1_ref)
        _top_rows(s2, k, v2_ref, i2_ref)
        v1, i1, v2, i2 = v1_ref[...], i1_ref[...], v2_ref[...], i2_ref[...]
        tok = v1.shape[1]
        spans = [(0, k)] + [(a, k // 2) for a in range(1, k // 2)]
        cand = [v1[a:a + 1] + v2[:nb] for a, nb in spans] + [v1[k // 2:] + v2[0:1]]
        cand_a = ([jnp.broadcast_to(i1[a:a + 1], (nb, tok)) for a, nb in spans] + [i1[k // 2:]])
        cand_b = [i2[:nb] for a, nb in spans] + [jnp.broadcast_to(i2[0:1], (k - k // 2, tok))]
        _top_rows(jnp.concatenate(cand, axis=0), k, tv_ref, None,
                  extra=((jnp.concatenate(cand_a, axis=0), ta_ref), (jnp.concatenate(cand_b, axis=0), tb_ref)))
        top = tv_ref[...]
        e = jnp.exp(top - jnp.max(top, axis=0, keepdims=True))
        gate = e / jnp.sum(e, axis=0, keepdims=True)
        row = pl.multiple_of(hd * k, k)
        ia_ref[pl.ds(row, k), :] = ta_ref[...]
        ib_ref[pl.ds(row, k), :] = tb_ref[...]
        gate_ref[pl.ds(row, k), :] = gate
        return carry

    lax.fori_loop(0, PEER_HEADS, head, 0)


def _peer_route(x2, gain, shift, scale, wqt, sk, seq):
    n, d = x2.shape
    t = ROUTE_TILE
    tiles_per_batch = seq // t
    const2 = lambda i: (0, 0)
    mod_spec = pl.BlockSpec((1, 1, d), lambda i: (i // tiles_per_batch, 0, 0))
    slot_spec = pl.BlockSpec((PEER_SLOTS, t), lambda i: (0, i))
    k = PEER_TOPK
    return pl.pallas_call(
        _peer_route_kernel,
        grid=(n // t,),
        in_specs=[
            pl.BlockSpec((t, d), lambda i: (i, 0)),
            pl.BlockSpec((1, d), const2),
            mod_spec, mod_spec,
            pl.BlockSpec(wqt.shape, const2),
            pl.BlockSpec(sk.shape, lambda i: (0, 0, 0)),
        ],
        out_specs=[slot_spec, slot_spec, slot_spec],
        out_shape=[
            jax.ShapeDtypeStruct((PEER_SLOTS, n), jnp.int32),
            jax.ShapeDtypeStruct((PEER_SLOTS, n), jnp.int32),
            jax.ShapeDtypeStruct((PEER_SLOTS, n), F32),
        ],
        scratch_shapes=[
            pltpu.VMEM((wqt.shape[0], t), F32),
            pltpu.VMEM((k, t), F32), pltpu.VMEM((k, t), jnp.int32),
            pltpu.VMEM((k, t), F32), pltpu.VMEM((k, t), jnp.int32),
            pltpu.VMEM((k, t), F32), pltpu.VMEM((k, t), jnp.int32), pltpu.VMEM((k, t), jnp.int32),
        ],
        compiler_params=_params("arbitrary"),
        name="peer_route",
    )(x2, gain, shift, scale, wqt, sk)


def _peer_dense_kernel(x_ref, gain_ref, shift_ref, scale_ref, gres_ref, ia_ref, ib_ref, gate_ref,
                       u_ref, v_ref, y_ref, w_ref, h_ref, acc_ref, iat_ref, ibt_ref, gt_ref):
    j = pl.program_id(1)
    t = x_ref.shape[0]
    nk = PEER_N_KEYS

    @pl.when(j == 0)
    def _():
        h_ref[...] = _norm_mod(x_ref[...], gain_ref[...], shift_ref[0], scale_ref[0]).astype(BF16)
        acc_ref[...] = jnp.zeros(acc_ref.shape, F32)
        iat_ref[...] = ia_ref[...].T
        ibt_ref[...] = ib_ref[...].T
        gt_ref[...] = gate_ref[...].T
        key = lax.broadcasted_iota(jnp.int32, (nk, PEER_SLOTS), 0)
        sub = 8

        def build(grp, carry):
            base = pl.multiple_of(grp * PEER_BUILD_GROUP, PEER_BUILD_GROUP)
            slabs = []
            for s0 in range(0, PEER_BUILD_GROUP, sub):
                grids = []
                for i in range(sub):
                    row = pl.ds(base + s0 + i, 1)
                    a_hot = jnp.where(key == iat_ref[row, :], 1.0, 0.0).astype(BF16)
                    b_gate = jnp.where(key == ibt_ref[row, :], gt_ref[row, :], 0.0).astype(BF16)
                    grids.append(_nt_dot(a_hot, b_gate))
                slabs.append(pltpu.einshape("tab->atb", jnp.stack(grids, axis=0)))
            w_ref[:, pl.ds(base, PEER_BUILD_GROUP), :] = jnp.concatenate(slabs, axis=1).astype(BF16)
            return carry

        lax.fori_loop(0, t // PEER_BUILD_GROUP, build, 0)

    act = _nt_dot(h_ref[...], u_ref[...])
    parts = []
    for al in range(PEER_A_CHUNK):
        w_a = w_ref[j * PEER_A_CHUNK + al]
        parts.append(w_a * jax.nn.gelu(act[:, al * nk:(al + 1) * nk]).astype(BF16))
    acc_ref[...] += _dot(jnp.concatenate(parts, axis=1), v_ref[...])

    @pl.when(j == pl.num_programs(1) - 1)
    def _():
        y_ref[...] = x_ref[...] + gres_ref[0] * acc_ref[...]


def _peer_dense(x2, gain, shift, scale, gres, ia, ib, gate, u, v, seq):
    n, d = x2.shape
    t = PEER_TILE
    tiles_per_batch = seq // t
    ec = PEER_A_CHUNK * PEER_N_KEYS
    n_exp = u.shape[0]
    mod_spec = pl.BlockSpec((1, 1, d), lambda i, j: (i // tiles_per_batch, 0, 0))
    slot_spec = pl.BlockSpec((PEER_SLOTS, t), lambda i, j: (0, i))
    return pl.pallas_call(
        _peer_dense_kernel,
        grid=(n // t, n_exp // ec),
        in_specs=[
            pl.BlockSpec((t, d), lambda i, j: (i, 0)),
            pl.BlockSpec((1, d), lambda i, j: (0, 0)),
            mod_spec, mod_spec, mod_spec,
            slot_spec, slot_spec, slot_spec,
            pl.BlockSpec((ec, d), lambda i, j: (j, 0)),
            pl.BlockSpec((ec, d), lambda i, j: (j, 0)),
        ],
        out_specs=pl.BlockSpec((t, d), lambda i, j: (i, 0)),
        out_shape=jax.ShapeDtypeStruct((n, d), F32),
        scratch_shapes=[
            pltpu.VMEM((PEER_N_KEYS, t, PEER_N_KEYS), BF16),
            pltpu.VMEM((t, d), BF16),
            pltpu.VMEM((t, d), F32),
            pltpu.VMEM((t, PEER_SLOTS), jnp.int32),
            pltpu.VMEM((t, PEER_SLOTS), jnp.int32),
            pltpu.VMEM((t, PEER_SLOTS), F32),
        ],
        compiler_params=_params("arbitrary", "arbitrary"),
        name="peer_dense",
    )(x2, gain, shift, scale, gres, ia, ib, gate, u, v)


def _peer_layer(x2, gain, shift, scale, gres, wqt, sk, u, v, seq):
    ia, ib, gate = _peer_route(x2, gain, shift, scale, wqt, sk, seq)
    return _peer_dense(x2, gain, shift, scale, gres, ia, ib, gate, u, v, seq)


def _kv_proj_kernel(x_ref, gain_ref, shift_ref, scale_ref, wn_ref, wvt_ref, kg_ref,
                    cmp_ref, k_ref, vt_ref):
    h = _norm_mod(x_ref[...], gain_ref[...], shift_ref[0], scale_ref[0]).astype(BF16)
    kvn = _dot(h, wn_ref[...])
    hb = HEAD_BLOCK
    for c in range(4):
        cmp_ref[c] = kvn[:, c * hb:(c + 1) * hb].astype(BF16)
    for c in range(4):
        kk = kvn[:, (4 + c) * hb:(5 + c) * hb]
        kn = kk * lax.rsqrt(jnp.mean(kk * kk, axis=-1, keepdims=True) + EPS) * kg_ref[pl.ds(c // 2, 1), :]
        k_ref[c] = kn.astype(BF16)
    vt = _nt_dot(wvt_ref[...], h)
    for c in range(4):
        for j in range(vt.shape[1] // ATT_TILE):
            vt_ref[c, j] = vt[c * hb:(c + 1) * hb, j * ATT_TILE:(j + 1) * ATT_TILE].astype(BF16)


def _kv_proj(x2, gain, shift, scale, wn, wvt, kg, seq):
    n, d = x2.shape
    tm = ROW_TILE
    tiles_per_batch = seq // tm
    const2 = lambda i: (0, 0)
    mod_spec = pl.BlockSpec((1, 1, d), lambda i: (i // tiles_per_batch, 0, 0))
    return pl.pallas_call(
        _kv_proj_kernel,
        grid=(n // tm,),
        in_specs=[
            pl.BlockSpec((tm, d), lambda i: (i, 0)),
            pl.BlockSpec((1, d), const2),
            mod_spec, mod_spec,
            pl.BlockSpec(wn.shape, const2),
            pl.BlockSpec(wvt.shape, const2),
            pl.BlockSpec(kg.shape, const2),
        ],
        out_specs=[
            pl.BlockSpec((4, tm, HEAD_BLOCK), lambda i: (0, i, 0)),
            pl.BlockSpec((4, tm, HEAD_BLOCK), lambda i: (0, i, 0)),
            pl.BlockSpec((4, tm // ATT_TILE, HEAD_BLOCK, ATT_TILE), lambda i: (0, i, 0, 0)),
        ],
        out_shape=[
            jax.ShapeDtypeStruct((4, n, HEAD_BLOCK), BF16),
            jax.ShapeDtypeStruct((4, n, HEAD_BLOCK), BF16),
            jax.ShapeDtypeStruct((4, n // ATT_TILE, HEAD_BLOCK, ATT_TILE), BF16),
        ],
        compiler_params=_params("arbitrary"),
        name="kv_proj",
    )(x2, gain, shift, scale, wn, wvt, kg)


def _compress_kernel(xk_ref, xv_ref, w1_ref, pos_ref, b1_ref, w2_ref, kg_ref, kc_ref, vct_ref):
    half = w1_ref.shape[1] // 2
    rows = xk_ref.shape[2]
    for kv, x_ref in ((0, xk_ref), (1, xv_ref)):
        x = x_ref[0, 0]
        w1 = w1_ref[kv]
        first = _dot(x, w1[:half])
        second = _dot(x, w1[half:])
        hid = first + pltpu.roll(second, rows - 1, axis=0)
        pos_term = _dot(jnp.broadcast_to(pos_ref[kv], (8, 2 * half)).astype(BF16), w1)[0:1]
        hid = jax.nn.gelu(hid + pos_term + b1_ref[kv])
        out = _dot(hid.astype(BF16), w2_ref[kv])
        if kv == 0:
            out = out * lax.rsqrt(jnp.mean(out * out, axis=-1, keepdims=True) + EPS) * kg_ref[...]
            kc_ref[0, 0] = out.astype(BF16)
        else:
            vct_ref[0, 0] = out.T.astype(BF16)


def _compress(xr, w1, pos, b1, w2, kg, batch):
    _, _, rows, width = xr.shape
    g = NSA_KV_GROUPS
    return pl.pallas_call(
        _compress_kernel,
        grid=(batch, g),
        in_specs=[
            pl.BlockSpec((1, 1, rows, width), lambda b, gi: (gi, b, 0, 0)),
            pl.BlockSpec((1, 1, rows, width), lambda b, gi: (g + gi, b, 0, 0)),
            pl.BlockSpec(w1.shape, lambda b, gi: (0, 0, 0)),
            pl.BlockSpec(pos.shape, lambda b, gi: (0, 0, 0)),
            pl.BlockSpec(b1.shape, lambda b, gi: (0, 0, 0)),
            pl.BlockSpec(w2.shape, lambda b, gi: (0, 0, 0)),
            pl.BlockSpec(kg.shape, lambda b, gi: (0, 0)),
        ],
        out_specs=[
            pl.BlockSpec((1, 1, rows, HEAD_BLOCK), lambda b, gi: (b, gi, 0, 0)),
            pl.BlockSpec((1, 1, HEAD_BLOCK, rows), lambda b, gi: (b, gi, 0, 0)),
        ],
        out_shape=[
            jax.ShapeDtypeStruct((batch, g, rows, HEAD_BLOCK), BF16),
            jax.ShapeDtypeStruct((batch, g, HEAD_BLOCK, rows), BF16),
        ],
        compiler_params=_params("arbitrary", "arbitrary"),
        name="compress",
    )(xr, xr, w1, pos, b1, w2, kg)


def _nsa_q_proj_kernel(x_ref, gain_ref, shift_ref, scale_ref, wq_ref, wgt_ref, bg_ref, qg_ref,
                       q_ref, gt_ref):
    h = _norm_mod(x_ref[...], gain_ref[...], shift_ref[0], scale_ref[0]).astype(BF16)
    q = _dot(h, wq_ref[...])
    hb = HEAD_BLOCK
    for c in range(N_HEADS):
        qq = q[:, c * hb:(c + 1) * hb]
        qn = qq * lax.rsqrt(jnp.mean(qq * qq, axis=-1, keepdims=True) + EPS) * qg_ref[...]
        q_ref[c] = qn.astype(BF16)
    gt_ref[...] = jax.nn.sigmoid(_nt_dot(wgt_ref[...], h) + bg_ref[...])


def _nsa_q_proj(x2, gain, shift, scale, wq, wgt, bg, qg, seq):
    n, d = x2.shape
    tm = ROW_TILE
    tiles_per_batch = seq // tm
    const2 = lambda i: (0, 0)
    mod_spec = pl.BlockSpec((1, 1, d), lambda i: (i // tiles_per_batch, 0, 0))
    rows = wgt.shape[0]
    return pl.pallas_call(
        _nsa_q_proj_kernel,
        grid=(n // tm,),
        in_specs=[
            pl.BlockSpec((tm, d), lambda i: (i, 0)),
            pl.BlockSpec((1, d), const2),
            mod_spec, mod_spec,
            pl.BlockSpec(wq.shape, const2),
            pl.BlockSpec(wgt.shape, const2),
            pl.BlockSpec(bg.shape, const2),
            pl.BlockSpec(qg.shape, const2),
        ],
        out_specs=[
            pl.BlockSpec((N_HEADS, tm, HEAD_BLOCK), lambda i: (0, i, 0)),
            pl.BlockSpec((rows, tm), lambda i: (0, i)),
        ],
        out_shape=[
            jax.ShapeDtypeStruct((N_HEADS, n, HEAD_BLOCK), BF16),
            jax.ShapeDtypeStruct((rows, n), F32),
        ],
        compiler_params=_params("arbitrary"),
        name="nsa_q_proj",
    )(x2, gain, shift, scale, wq, wgt, bg, qg)


def _nsa_attn_kernel(n_slc_bias, n_win_bias, top_n,
                     q_ref, g_ref, kc_ref, vct_ref, ks_ref, vst_ref, kw_ref, vwt_ref,
                     cb_ref, sb_ref, wb_ref, ov_ref, o_ref,
                     sel_ref, oc_ref, m_ref, l_ref, acc_ref, os_ref):
    qi = pl.program_id(2)
    nq = pl.num_programs(2)
    t = ATT_TILE
    hg = NSA_HEADS_PER_GROUP
    q = q_ref[...].reshape(hg * t, HEAD_BLOCK)
    n_cmp_rows = kc_ref.shape[2]
    n_blk = ov_ref.shape[0]

    off = pl.multiple_of((nq - 1 - qi) * (t // CMP_STRIDE), t // CMP_STRIDE)
    bias_c = cb_ref[0, pl.ds(off, n_cmp_rows), :]
    sc = _nt_dot(kc_ref[0, 0], q) + bias_c
    visible = bias_c > 0.5 * NEG_INF
    e = jnp.where(visible, jnp.exp2(sc - jnp.max(sc, axis=0, keepdims=True)), 0.0)
    lsum = jnp.sum(e, axis=0, keepdims=True)
    p = e / jnp.where(lsum > 0.0, lsum, 1.0)
    oc_ref[...] = _dot(vct_ref[0, 0], p.astype(BF16))

    psum = p[:, 0:t]
    for i in range(1, hg):
        psum = psum + p[:, i * t:(i + 1) * t]
    p_hi = psum.astype(BF16)
    p_lo = (psum - p_hi.astype(F32)).astype(BF16)
    imp = _dot(ov_ref[...], p_hi) + _dot(ov_ref[...], p_lo)
    blk = lax.broadcasted_iota(jnp.int32, imp.shape, 0)
    pos = qi * t + lax.broadcasted_iota(jnp.int32, imp.shape, 1)
    back = pos // SEL_BLOCK - blk
    forced = (blk == 0) | ((back >= 0) & (back < SEL_N_LOCAL))
    imp = jnp.where(forced, FORCE, imp)
    imp = jnp.where(back >= 0, imp, NEG_INF)
    sel = jnp.zeros(imp.shape, F32)
    for _ in range(top_n):
        m = jnp.max(imp, axis=0, keepdims=True)
        idx = jnp.min(jnp.where(imp == m, blk, n_blk), axis=0, keepdims=True)
        hit = blk == idx
        sel = jnp.where(hit & (m >= 0.0), 1.0, sel)
        imp = jnp.where(hit, -jnp.inf, imp)
    sel_ref[...] = jnp.where(sel > 0.0, 0.0, NEG_INF)

    blocks_per_tile = t // SEL_BLOCK

    _softmax_init(m_ref, l_ref, acc_ref)

    def slc_body(kt, carry):
        k = ks_ref[0, pl.ds(pl.multiple_of(kt * t, t), t), :]
        bias = sb_ref[0, jnp.minimum(qi - kt, n_slc_bias - 1)]
        rows = [jnp.broadcast_to(sel_ref[pl.ds(kt * blocks_per_tile + jj, 1), :], (SEL_BLOCK, t))
                for jj in range(blocks_per_tile)]
        mask = jnp.concatenate(rows, axis=0)
        s = _nt_dot(k, q) + bias + jnp.concatenate([mask] * hg, axis=1)
        _online_softmax_step(s, vst_ref[0, kt], m_ref, l_ref, acc_ref)
        return carry

    lax.fori_loop(0, qi + 1, slc_body, 0)
    os_ref[...] = acc_ref[...] / l_ref[...]

    _softmax_init(m_ref, l_ref, acc_ref)

    def win_body(kt, carry):
        k = kw_ref[0, pl.ds(pl.multiple_of(kt * t, t), t), :]
        s = _nt_dot(k, q) + wb_ref[0, qi - kt]
        _online_softmax_step(s, vwt_ref[0, kt], m_ref, l_ref, acc_ref)
        return carry

    lax.fori_loop(jnp.maximum(qi - (n_win_bias - 1), 0), qi + 1, win_body, 0)
    ow = acc_ref[...] / l_ref[...]

    gates = g_ref[...]
    for i in range(hg):
        sl = slice(i * t, (i + 1) * t)
        o = (gates[3 * i:3 * i + 1] * oc_ref[:, sl] + gates[3 * i + 1:3 * i + 2] * os_ref[:, sl]
             + gates[3 * i + 2:3 * i + 3] * ow[:, sl])
        o_ref[:, i * HEAD_BLOCK:(i + 1) * HEAD_BLOCK] = o.T.astype(BF16)


def _nsa_attn(q, gt, kc, vct, k, vt, cmp_bias, slc_bias, win_bias, ovt, batch, seq):
    t = ATT_TILE
    nq = seq // t
    hg = NSA_HEADS_PER_GROUP
    g = NSA_KV_GROUPS
    n = batch * seq
    top_n = min(SEL_TOPN, seq // SEL_BLOCK)
    lanes = hg * t
    idx3 = lambda b, gi, i: (gi, 0, 0)
    idx4 = lambda b, gi, i: (gi, 0, 0, 0)
    return pl.pallas_call(
        functools.partial(_nsa_attn_kernel, slc_bias.shape[1], win_bias.shape[1], top_n),
        grid=(batch, g, nq),
        in_specs=[
            pl.BlockSpec((hg, t, HEAD_BLOCK), lambda b, gi, i: (gi, b * nq + i, 0)),
            pl.BlockSpec((GATE_ROWS, t), lambda b, gi, i: (gi, b * nq + i)),
            pl.BlockSpec((1, 1) + kc.shape[2:], lambda b, gi, i: (b, gi, 0, 0)),
            pl.BlockSpec((1, 1) + vct.shape[2:], lambda b, gi, i: (b, gi, 0, 0)),
            pl.BlockSpec((1, seq, HEAD_BLOCK), lambda b, gi, i: (gi, b, 0)),
            pl.BlockSpec((1, nq, HEAD_BLOCK, t), lambda b, gi, i: (gi, b, 0, 0)),
            pl.BlockSpec((1, seq, HEAD_BLOCK), lambda b, gi, i: (g + gi, b, 0)),
            pl.BlockSpec((1, nq, HEAD_BLOCK, t), lambda b, gi, i: (g + gi, b, 0, 0)),
            pl.BlockSpec((1,) + cmp_bias.shape[1:], idx3),
            pl.BlockSpec((1,) + slc_bias.shape[1:], idx4),
            pl.BlockSpec((1,) + win_bias.shape[1:], idx4),
            pl.BlockSpec(ovt.shape, lambda b, gi, i: (0, 0)),
        ],
        out_specs=pl.BlockSpec((t, hg * HEAD_BLOCK), lambda b, gi, i: (b * nq + i, gi)),
        out_shape=jax.ShapeDtypeStruct((n, N_HEADS * HEAD_BLOCK), BF16),
        scratch_shapes=[
            pltpu.VMEM((ovt.shape[0], t), F32),
            pltpu.VMEM((HEAD_BLOCK, lanes), F32),
            pltpu.VMEM((1, lanes), F32),
            pltpu.VMEM((1, lanes), F32),
            pltpu.VMEM((HEAD_BLOCK, lanes), F32),
            pltpu.VMEM((HEAD_BLOCK, lanes), F32),
        ],
        compiler_params=_params("arbitrary", "arbitrary", "arbitrary"),
        name="nsa_attn",
    )(q, gt, kc, vct, k, vt, k, vt, cmp_bias, slc_bias, win_bias, ovt)


def _skew_rows(v, rows):
    heads, length = v.shape
    ext = jnp.concatenate([v, jnp.zeros((heads, 1), v.dtype)], axis=1)
    rep = jnp.broadcast_to(ext[:, None, :], (heads, rows, length + 1)).reshape(heads, rows * (length + 1))
    return rep[:, :rows * length].reshape(heads, rows, length)


def _bias_tiles(profile, t, n_tiles, window=None):
    heads = profile.shape[0]
    span = n_tiles * t
    vis = profile[:, :span]
    if window is not None:
        vis = jnp.where(jnp.arange(span)[None] < window, vis, NEG_INF)
    v = jnp.concatenate([jnp.full((heads, t), NEG_INF, F32), vis], axis=1)
    m = _skew_rows(v, t)[:, :, t:]
    return m.reshape(heads, t, n_tiles, t).transpose(0, 2, 1, 3)


def _cmp_bias_strip(profile, seq):
    t = ATT_TILE
    heads = profile.shape[0]
    per = t // CMP_STRIDE
    n_rho = 2 * (seq // t) - 1
    c = seq - t - (CMP_LEN - 1)
    pad = t * (n_rho - 1) + CMP_STRIDE * (per - 1) - c
    g = jnp.concatenate([jnp.full((heads, pad), NEG_INF, F32), profile], axis=1)
    slabs = []
    for sigma in range(per):
        base = c - CMP_STRIDE * sigma + pad
        seg = g[:, base - t * (n_rho - 1):base + t].reshape(heads, n_rho, t)
        slabs.append(seg[:, ::-1])
    return jnp.stack(slabs, axis=2).reshape(heads, n_rho * per, t)


def _n_far_tiles(t):
    return -(-(REL_MAX_DIST + t - 1) // t) + 1


def _group_lanes(tiles):
    hg = NSA_HEADS_PER_GROUP
    parts = tiles.reshape((NSA_KV_GROUPS, hg) + tiles.shape[1:])
    return jnp.concatenate([parts[:, i] for i in range(hg)], axis=-1)


def kernel(x, c, rel_bias, ada_w, ada_b, norm_mix, norm_ffn, diff_w_in, diff_w_out, diff_lambda,
           diff_q_gain, diff_k_gain, diff_subln, kv_norm, kv_ada_w, kv_ada_b, kv_w, cmp_pos, cmp_w1,
           cmp_b1, cmp_w2, nsa_k_gain, nsa_w_in, nsa_b_gate, nsa_w_out, nsa_q_gain, peer_w_q,
           peer_subkeys, peer_u, peer_v):
    batch, seq, d = x.shape
    n = batch * seq
    t = ATT_TILE
    nq = seq // t
    width = N_HEADS * HEAD_BLOCK
    x2 = x.reshape(n, d)

    c_pad = jnp.pad(c, ((0, 8 - batch % 8 if batch % 8 else 0), (0, 0)))
    mods = _ada(c_pad, ada_w, ada_b, 1536)
    kv_mods = _ada(c_pad, kv_ada_w[None], kv_ada_b[None], 1024)

    def mod(arr, layer, k):
        return arr[layer, :batch, k * d:(k + 1) * d].reshape(batch, 1, d)

    profile = rel_bias[_rel_bucket(jnp.arange(seq))].T * LOG2_E

    lam_init = 0.8 - 0.6 * math.exp(-0.3 * 0)
    w_in = diff_w_in[0]
    wqk = w_in[:, :2 * width].astype(BF16)
    wvt = w_in[:, 2 * width:].T.astype(BF16)
    grp = jnp.asarray(np.kron(np.eye(width // DIFF_HEAD_DIM), np.ones((DIFF_HEAD_DIM, DIFF_HEAD_DIM))), BF16)
    reps = width // DIFF_HEAD_DIM
    qg = (jnp.tile(diff_q_gain[0], reps) * (DIFF_HEAD_DIM ** -0.5 * LOG2_E)).reshape(1, width)
    kg = jnp.tile(diff_k_gain[0], reps).reshape(1, width)
    q, k, vt = _diff_proj(x2, norm_mix[0:1], mod(mods, 0, 0), mod(mods, 0, 1), wqk, wvt, grp, qg, kg, seq)
    diff_bias = _bias_tiles(profile, DIFF_TILE, min(_n_far_tiles(DIFF_TILE), seq // DIFF_TILE))
    sub_col = (diff_subln[0] * (1.0 - lam_init)).reshape(HEAD_BLOCK, 1)
    o = _diff_attn(q, k, vt, diff_bias, diff_lambda[0], sub_col, batch, seq, lam_init)
    x2 = _out_proj(x2, o, diff_w_out[0].astype(BF16), mod(mods, 0, 2), seq)

    def peer(x2, layer):
        return _peer_layer(x2, norm_ffn[layer:layer + 1], mod(mods, layer, 3), mod(mods, layer, 4),
                           mod(mods, layer, 5), peer_w_q[layer].T.astype(BF16),
                           peer_subkeys[layer].astype(BF16), peer_u[layer].astype(BF16),
                           peer_v[layer].astype(BF16), seq)

    x2 = peer(x2, 0)

    hb = HEAD_BLOCK

    def kv_cols(branch, kv, g):
        start = branch * 4 * hb + kv * 2 * hb + g * hb
        return kv_w[:, start:start + hb]

    wn = jnp.concatenate([kv_cols(0, 0, 0), kv_cols(0, 0, 1), kv_cols(0, 1, 0), kv_cols(0, 1, 1),
                          kv_cols(1, 0, 0), kv_cols(1, 0, 1), kv_cols(2, 0, 0), kv_cols(2, 0, 1)],
                         axis=1).astype(BF16)
    wvt_kv = jnp.concatenate([kv_cols(1, 1, 0), kv_cols(1, 1, 1), kv_cols(2, 1, 0), kv_cols(2, 1, 1)],
                             axis=1).T.astype(BF16)
    cmp_raw, k_sw, vt_sw = _kv_proj(x2, kv_norm.reshape(1, d), mod(kv_mods, 0, 0), mod(kv_mods, 0, 1),
                                    wn, wvt_kv, nsa_k_gain[1:3], seq)
    xr = cmp_raw.reshape(4, batch, seq // CMP_STRIDE, CMP_STRIDE * hb)
    kc, vct = _compress(xr, cmp_w1.astype(BF16), cmp_pos.reshape(2, 1, CMP_LEN * hb),
                        cmp_b1.reshape(2, 1, CMP_HIDDEN), cmp_w2.astype(BF16), nsa_k_gain[0:1], batch)

    hg = NSA_HEADS_PER_GROUP
    w_in1 = nsa_w_in[0]
    wq = w_in1[:, :width].astype(BF16)
    w_gate = w_in1[:, width:].reshape(d, NSA_KV_GROUPS, hg * N_BRANCH)
    w_gate = jnp.pad(w_gate, ((0, 0), (0, 0), (0, GATE_ROWS - hg * N_BRANCH)))
    wgt = w_gate.reshape(d, NSA_KV_GROUPS * GATE_ROWS).T.astype(BF16)
    b_gate = jnp.pad(nsa_b_gate[0].reshape(NSA_KV_GROUPS, hg * N_BRANCH),
                     ((0, 0), (0, GATE_ROWS - hg * N_BRANCH))).reshape(NSA_KV_GROUPS * GATE_ROWS, 1)
    qg1 = (nsa_q_gain[0] * (hb ** -0.5 * LOG2_E)).reshape(1, hb)
    q1, gt = _nsa_q_proj(x2, norm_mix[1:2], mod(mods, 1, 0), mod(mods, 1, 1), wq, wgt, b_gate, qg1, seq)

    slc_bias = _group_lanes(_bias_tiles(profile, t, min(_n_far_tiles(t), nq)))
    win_bias = _group_lanes(_bias_tiles(profile, t, min(WINDOW // t + 1, nq), window=WINDOW))
    cmp_bias = _group_lanes(_cmp_bias_strip(profile, seq))
    n_rows = seq // CMP_STRIDE
    n_sel = seq // SEL_BLOCK
    cmp_start = np.arange(n_rows) * CMP_STRIDE
    sel_start = np.arange(n_sel) * SEL_BLOCK
    overlap = np.clip(np.minimum(cmp_start[:, None] + CMP_LEN, sel_start[None, :] + SEL_BLOCK)
                      - np.maximum(cmp_start[:, None], sel_start[None, :]), 0, None) / CMP_LEN
    overlap[(seq - CMP_LEN) // CMP_STRIDE + 1:] = 0.0
    ovt = jnp.asarray(overlap.T, BF16)
    o1 = _nsa_attn(q1, gt, kc, vct, k_sw, vt_sw, cmp_bias, slc_bias, win_bias, ovt, batch, seq)
    x2 = _out_proj(x2, o1, nsa_w_out[0].astype(BF16), mod(mods, 1, 2), seq)

    x2 = peer(x2, 1)
    return x2.reshape(batch, seq, d)
```

```python
import functools
import math

import jax
import jax.numpy as jnp
import numpy as np
from jax import lax
from jax.experimental import pallas as pl
from jax.experimental.pallas import tpu as pltpu

F32 = jnp.float32
BF16 = jnp.bfloat16

N_HEADS = 8
N_ADA = 6
EPS = 1e-6
NEG_INF = -1e30
FORCE = 1e9
LOG2_E = math.log2(math.e)

DIFF_HEAD_DIM = 64
HEAD_BLOCK = 128

NSA_KV_GROUPS = 2
NSA_HEADS_PER_GROUP = N_HEADS // NSA_KV_GROUPS
N_BRANCH = 3
CMP_LEN = 32
CMP_STRIDE = 16
CMP_HIDDEN = 256
SEL_BLOCK = 64
SEL_TOPN = 16
SEL_N_LOCAL = 2
WINDOW = 512
GATE_ROWS = 16

REL_BUCKETS = 32
REL_MAX_DIST = 1024

PEER_HEADS = 8
PEER_N_KEYS = 128
PEER_TOPK = 16
PEER_SLOTS = PEER_HEADS * PEER_TOPK

ROW_TILE = 512
ATT_TILE = 256
DIFF_TILE = 512
ROUTE_TILE = 512
PEER_TILE = 512
PEER_BUILD_GROUP = 16
PEER_A_CHUNK = 8
VMEM_LIMIT = 56 * 1024 * 1024

NT_DIMS = (((1,), (1,)), ((), ()))


def _params(*sem):
    return pltpu.CompilerParams(dimension_semantics=sem, vmem_limit_bytes=VMEM_LIMIT)


def _nt_dot(a, b):
    return lax.dot_general(a, b, NT_DIMS, preferred_element_type=F32)


def _dot(a, b):
    return jnp.dot(a, b, preferred_element_type=F32)


def _rel_bucket(dist):
    max_exact = REL_BUCKETS // 2
    n = jnp.maximum(dist, 0)
    nf = jnp.maximum(n, 1).astype(jnp.float32)
    large = max_exact + (jnp.log(nf / max_exact) / math.log(REL_MAX_DIST / max_exact)
                         * (REL_BUCKETS - max_exact)).astype(jnp.int32)
    large = jnp.minimum(large, REL_BUCKETS - 1)
    return jnp.where(n < max_exact, n, large)


def _norm_mod(x, gain, shift, scale):
    y = x * lax.rsqrt(jnp.mean(x * x, axis=-1, keepdims=True) + EPS) * gain
    return y * (1.0 + scale) + shift


def _ada_kernel(c_ref, w_ref, b_ref, o_ref):
    c = c_ref[...]
    ca = c * jax.nn.sigmoid(c)
    o_ref[0] = jnp.dot(ca, w_ref[0], preferred_element_type=F32,
                       precision=lax.Precision.HIGHEST) + b_ref[0]


def _ada(c_pad, w, b, col_tile):
    layers, d, n = w.shape
    rows = c_pad.shape[0]
    return pl.pallas_call(
        _ada_kernel,
        grid=(layers, n // col_tile),
        in_specs=[
            pl.BlockSpec((rows, d), lambda l, j: (0, 0)),
            pl.BlockSpec((1, d, col_tile), lambda l, j: (l, 0, j)),
            pl.BlockSpec((1, 1, col_tile), lambda l, j: (l, 0, j)),
        ],
        out_specs=pl.BlockSpec((1, rows, col_tile), lambda l, j: (l, 0, j)),
        out_shape=jax.ShapeDtypeStruct((layers, rows, n), F32),
        compiler_params=_params("arbitrary", "arbitrary"),
        name="ada",
    )(c_pad, w, b.reshape(layers, 1, n))


def _diff_proj_kernel(x_ref, gain_ref, shift_ref, scale_ref, wqk_ref, wvt_ref, grp_ref,
                      qg_ref, kg_ref, q_ref, k_ref, vt_ref):
    h = _norm_mod(x_ref[...], gain_ref[...], shift_ref[0], scale_ref[0]).astype(BF16)
    qk = _dot(h, wqk_ref[...])
    width = N_HEADS * HEAD_BLOCK
    for part, out_ref, g_ref in ((0, q_ref, qg_ref), (1, k_ref, kg_ref)):
        z = qk[:, part * width:(part + 1) * width]
        zz = z * z
        hi = zz.astype(BF16)
        lo = (zz - hi.astype(F32)).astype(BF16)
        ss = _dot(hi, grp_ref[...]) + _dot(lo, grp_ref[...])
        zn = z * lax.rsqrt(ss * (1.0 / DIFF_HEAD_DIM) + EPS) * g_ref[...]
        for c in range(N_HEADS):
            out_ref[c] = zn[:, c * HEAD_BLOCK:(c + 1) * HEAD_BLOCK].astype(BF16)
    vt = _nt_dot(wvt_ref[...], h)
    n_chunk = vt.shape[1] // DIFF_TILE
    for c in range(N_HEADS):
        for j in range(n_chunk):
            vt_ref[c, j] = vt[c * HEAD_BLOCK:(c + 1) * HEAD_BLOCK,
                              j * DIFF_TILE:(j + 1) * DIFF_TILE].astype(BF16)


def _diff_proj(x2, gain, shift, scale, wqk, wvt, grp, qg, kg, seq):
    n, d = x2.shape
    tm = ROW_TILE
    tiles_per_batch = seq // tm
    width = N_HEADS * HEAD_BLOCK
    const2 = lambda i: (0, 0)
    mod_spec = pl.BlockSpec((1, 1, d), lambda i: (i // tiles_per_batch, 0, 0))
    return pl.pallas_call(
        _diff_proj_kernel,
        grid=(n // tm,),
        in_specs=[
            pl.BlockSpec((tm, d), lambda i: (i, 0)),
            pl.BlockSpec((1, d), const2),
            mod_spec, mod_spec,
            pl.BlockSpec((d, 2 * width), const2),
            pl.BlockSpec((width, d), const2),
            pl.BlockSpec((width, width), const2),
            pl.BlockSpec((1, width), const2),
            pl.BlockSpec((1, width), const2),
        ],
        out_specs=[
            pl.BlockSpec((N_HEADS, tm, HEAD_BLOCK), lambda i: (0, i, 0)),
            pl.BlockSpec((N_HEADS, tm, HEAD_BLOCK), lambda i: (0, i, 0)),
            pl.BlockSpec((N_HEADS, tm // DIFF_TILE, HEAD_BLOCK, DIFF_TILE), lambda i: (0, i, 0, 0)),
        ],
        out_shape=[
            jax.ShapeDtypeStruct((N_HEADS, n, HEAD_BLOCK), BF16),
            jax.ShapeDtypeStruct((N_HEADS, n, HEAD_BLOCK), BF16),
            jax.ShapeDtypeStruct((N_HEADS, n // DIFF_TILE, HEAD_BLOCK, DIFF_TILE), BF16),
        ],
        compiler_params=_params("arbitrary"),
        name="diff_proj",
    )(x2, gain, shift, scale, wqk, wvt, grp, qg, kg)


def _online_softmax_step(s, vt, m_ref, l_ref, acc_ref):
    m_old = m_ref[...]
    m_new = jnp.maximum(m_old, jnp.max(s, axis=0, keepdims=True))
    alpha = jnp.exp2(m_old - m_new)
    p = jnp.exp2(s - m_new)
    l_ref[...] = alpha * l_ref[...] + jnp.sum(p, axis=0, keepdims=True)
    acc_ref[...] = alpha * acc_ref[...] + _dot(vt, p.astype(BF16))
    m_ref[...] = m_new


def _softmax_init(m_ref, l_ref, acc_ref):
    m_ref[...] = jnp.full(m_ref.shape, NEG_INF, F32)
    l_ref[...] = jnp.zeros(l_ref.shape, F32)
    acc_ref[...] = jnp.zeros(acc_ref.shape, F32)


def _diff_attn_kernel(lam_init, n_bias, q_ref, k_ref, vt_ref, bias_ref, lam_ref, sub_ref,
                      o_ref, m_ref, l_ref, acc_ref):
    qi = pl.program_id(2)
    t = DIFF_TILE
    q = q_ref[0]
    lane = lax.broadcasted_iota(jnp.int32, q.shape, 1)
    zero = jnp.zeros_like(q)
    qs = jnp.concatenate([jnp.where(lane < DIFF_HEAD_DIM, q, zero),
                          jnp.where(lane >= DIFF_HEAD_DIM, q, zero)], axis=0)
    _softmax_init(m_ref, l_ref, acc_ref)

    def body(kt, carry):
        k = k_ref[0, pl.ds(pl.multiple_of(kt * t, t), t), :]
        bias = bias_ref[0, jnp.minimum(qi - kt, n_bias - 1)]
        s = _nt_dot(k, qs) + jnp.concatenate([bias, bias], axis=1)
        _online_softmax_step(s, vt_ref[0, kt], m_ref, l_ref, acc_ref)
        return carry

    lax.fori_loop(0, qi + 1, body, 0)

    lam = lam_ref[...]
    lam_full = (jnp.exp(jnp.sum(lam[0:1] * lam[1:2], axis=1, keepdims=True))
                - jnp.exp(jnp.sum(lam[2:3] * lam[3:4], axis=1, keepdims=True)) + lam_init)
    o = acc_ref[...] / l_ref[...]
    o = o[:, :t] - lam_full * o[:, t:]
    y = o * lax.rsqrt(jnp.mean(o * o, axis=0, keepdims=True) + EPS) * sub_ref[...]
    o_ref[...] = y.T.astype(BF16)


def _diff_attn(q, k, vt, bias, lam, sub_col, batch, seq, lam_init):
    t = DIFF_TILE
    nq = seq // t
    n_bias = bias.shape[1]
    n = batch * seq
    return pl.pallas_call(
        functools.partial(_diff_attn_kernel, lam_init, n_bias),
        grid=(batch, N_HEADS, nq),
        in_specs=[
            pl.BlockSpec((1, t, HEAD_BLOCK), lambda b, h, i: (h, b * nq + i, 0)),
            pl.BlockSpec((1, seq, HEAD_BLOCK), lambda b, h, i: (h, b, 0)),
            pl.BlockSpec((1, nq, HEAD_BLOCK, t), lambda b, h, i: (h, b, 0, 0)),
            pl.BlockSpec((1, n_bias, t, t), lambda b, h, i: (h, 0, 0, 0)),
            pl.BlockSpec(lam.shape, lambda b, h, i: (0, 0)),
            pl.BlockSpec(sub_col.shape, lambda b, h, i: (0, 0)),
        ],
        out_specs=pl.BlockSpec((t, HEAD_BLOCK), lambda b, h, i: (b * nq + i, h)),
        out_shape=jax.ShapeDtypeStruct((n, N_HEADS * HEAD_BLOCK), BF16),
        scratch_shapes=[
            pltpu.VMEM((1, 2 * t), F32),
            pltpu.VMEM((1, 2 * t), F32),
            pltpu.VMEM((HEAD_BLOCK, 2 * t), F32),
        ],
        compiler_params=_params("arbitrary", "arbitrary", "arbitrary"),
        name="diff_attn",
    )(q, k, vt, bias, lam, sub_col)


def _out_proj_kernel(x_ref, o_ref, w_ref, g_ref, y_ref):
    y_ref[...] = x_ref[...] + g_ref[0] * _dot(o_ref[...], w_ref[...])


def _out_proj(x2, o, w, gate, seq):
    n, d = x2.shape
    tm = ROW_TILE
    tiles_per_batch = seq // tm
    return pl.pallas_call(
        _out_proj_kernel,
        grid=(n // tm,),
        in_specs=[
            pl.BlockSpec((tm, d), lambda i: (i, 0)),
            pl.BlockSpec((tm, o.shape[1]), lambda i: (i, 0)),
            pl.BlockSpec(w.shape, lambda i: (0, 0)),
            pl.BlockSpec((1, 1, d), lambda i: (i // tiles_per_batch, 0, 0)),
        ],
        out_specs=pl.BlockSpec((tm, d), lambda i: (i, 0)),
        out_shape=jax.ShapeDtypeStruct((n, d), F32),
        compiler_params=_params("arbitrary"),
        name="out_proj",
    )(x2, o, w, gate)


def _top_rows(s, count, val_ref, idx_ref, extra=()):
    rows = s.shape[0]
    iota = lax.broadcasted_iota(jnp.int32, s.shape, 0)
    for r in range(count):
        m = jnp.max(s, axis=0, keepdims=True)
        idx = jnp.min(jnp.where(s == m, iota, rows), axis=0, keepdims=True)
        hit = iota == idx
        val_ref[pl.ds(r, 1), :] = m
        if idx_ref is not None:
            idx_ref[pl.ds(r, 1), :] = idx
        for table, ref in extra:
            ref[pl.ds(r, 1), :] = jnp.max(jnp.where(hit, table, -1), axis=0, keepdims=True)
        s = jnp.where(hit, -jnp.inf, s)


def _peer_route_kernel(x_ref, gain_ref, shift_ref, scale_ref, wqt_ref, sk_ref,
                       ia_ref, ib_ref, gate_ref,
                       qt_ref, v1_ref, i1_ref, v2_ref, i2_ref, tv_ref, ta_ref, tb_ref):
    h = _norm_mod(x_ref[...], gain_ref[...], shift_ref[0], scale_ref[0]).astype(BF16)
    qt_ref[...] = _nt_dot(wqt_ref[...], h)
    k = PEER_TOPK
    half = PEER_N_KEYS

    def head(hd, carry):
        base = pl.multiple_of(hd * 2 * half, 2 * half)
        s1 = _dot(sk_ref[0], qt_ref[pl.ds(base, half), :].astype(BF16))
        s2 = _dot(sk_ref[1], qt_ref[pl.ds(base + half, half), :].astype(BF16))
        _top_rows(s1, k, v1_ref, i1_ref)
        _top_rows(s2, k, v2_ref, i2_ref)
        v1, i1, v2, i2 = v1_ref[...], i1_ref[...], v2_ref[...], i2_ref[...]
        tok = v1.shape[1]
        spans = [(0, k)] + [(a, k // 2) for a in range(1, k // 2)]
        cand = [v1[a:a + 1] + v2[:nb] for a, nb in spans] + [v1[k // 2:] + v2[0:1]]
        cand_a = ([jnp.broadcast_to(i1[a:a + 1], (nb, tok)) for a, nb in spans] + [i1[k // 2:]])
        cand_b = [i2[:nb] for a, nb in spans] + [jnp.broadcast_to(i2[0:1], (k - k // 2, tok))]
        _top_rows(jnp.concatenate(cand, axis=0), k, tv_ref, None,
                  extra=((jnp.concatenate(cand_a, axis=0), ta_ref), (jnp.concatenate(cand_b, axis=0), tb_ref)))
        top = tv_ref[...]
        e = jnp.exp(top - jnp.max(top, axis=0, keepdims=True))
        gate = e / jnp.sum(e, axis=0, keepdims=True)
        row = pl.multiple_of(hd * k, k)
        ia_ref[pl.ds(row, k), :] = ta_ref[...]
        ib_ref[pl.ds(row, k), :] = tb_ref[...]
        gate_ref[pl.ds(row, k), :] = gate
        return carry

    lax.fori_loop(0, PEER_HEADS, head, 0)


def _peer_route(x2, gain, shift, scale, wqt, sk, seq):
    n, d = x2.shape
    t = ROUTE_TILE
    tiles_per_batch = seq // t
    const2 = lambda i: (0, 0)
    mod_spec = pl.BlockSpec((1, 1, d), lambda i: (i // tiles_per_batch, 0, 0))
    slot_spec = pl.BlockSpec((PEER_SLOTS, t), lambda i: (0, i))
    k = PEER_TOPK
    return pl.pallas_call(
        _peer_route_kernel,
        grid=(n // t,),
        in_specs=[
            pl.BlockSpec((t, d), lambda i: (i, 0)),
            pl.BlockSpec((1, d), const2),
            mod_spec, mod_spec,
            pl.BlockSpec(wqt.shape, const2),
            pl.BlockSpec(sk.shape, lambda i: (0, 0, 0)),
        ],
        out_specs=[slot_spec, slot_spec, slot_spec],
        out_shape=[
            jax.ShapeDtypeStruct((PEER_SLOTS, n), jnp.int32),
            jax.ShapeDtypeStruct((PEER_SLOTS, n), jnp.int32),
            jax.ShapeDtypeStruct((PEER_SLOTS, n), F32),
        ],
        scratch_shapes=[
            pltpu.VMEM((wqt.shape[0], t), F32),
            pltpu.VMEM((k, t), F32), pltpu.VMEM((k, t), jnp.int32),
            pltpu.VMEM((k, t), F32), pltpu.VMEM((k, t), jnp.int32),
            pltpu.VMEM((k, t), F32), pltpu.VMEM((k, t), jnp.int32), pltpu.VMEM((k, t), jnp.int32),
        ],
        compiler_params=_params("arbitrary"),
        name="peer_route",
    )(x2, gain, shift, scale, wqt, sk)


def _peer_dense_kernel(x_ref, gain_ref, shift_ref, scale_ref, gres_ref, ia_ref, ib_ref, gate_ref,
                       u_ref, v_ref, y_ref, w_ref, h_ref, acc_ref, iat_ref, ibt_ref, gt_ref):
    j = pl.program_id(1)
    t = x_ref.shape[0]
    nk = PEER_N_KEYS

    @pl.when(j == 0)
    def _():
        h_ref[...] = _norm_mod(x_ref[...], gain_ref[...], shift_ref[0], scale_ref[0]).astype(BF16)
        acc_ref[...] = jnp.zeros(acc_ref.shape, F32)
        iat_ref[...] = ia_ref[...].T
        ibt_ref[...] = ib_ref[...].T
        gt_ref[...] = gate_ref[...].T
        key = lax.broadcasted_iota(jnp.int32, (nk, PEER_SLOTS), 0)
        sub = 8

        def build(grp, carry):
            base = pl.multiple_of(grp * PEER_BUILD_GROUP, PEER_BUILD_GROUP)
            slabs = []
            for s0 in range(0, PEER_BUILD_GROUP, sub):
                grids = []
                for i in range(sub):
                    row = pl.ds(base + s0 + i, 1)
                    a_hot = jnp.where(key == iat_ref[row, :], 1.0, 0.0).astype(BF16)
                    b_gate = jnp.where(key == ibt_ref[row, :], gt_ref[row, :], 0.0).astype(BF16)
                    grids.append(_nt_dot(a_hot, b_gate))
                slabs.append(pltpu.einshape("tab->atb", jnp.stack(grids, axis=0)))
            w_ref[:, pl.ds(base, PEER_BUILD_GROUP), :] = jnp.concatenate(slabs, axis=1).astype(BF16)
            return carry

        lax.fori_loop(0, t // PEER_BUILD_GROUP, build, 0)

    act = _nt_dot(h_ref[...], u_ref[...])
    parts = []
    for al in range(PEER_A_CHUNK):
        w_a = w_ref[j * PEER_A_CHUNK + al]
        parts.append(w_a * jax.nn.gelu(act[:, al * nk:(al + 1) * nk]).astype(BF16))
    acc_ref[...] += _dot(jnp.concatenate(parts, axis=1), v_ref[...])

    @pl.when(j == pl.num_programs(1) - 1)
    def _():
        y_ref[...] = x_ref[...] + gres_ref[0] * acc_ref[...]


def _peer_dense(x2, gain, shift, scale, gres, ia, ib, gate, u, v, seq):
    n, d = x2.shape
    t = PEER_TILE
    tiles_per_batch = seq // t
    ec = PEER_A_CHUNK * PEER_N_KEYS
    n_exp = u.shape[0]
    mod_spec = pl.BlockSpec((1, 1, d), lambda i, j: (i // tiles_per_batch, 0, 0))
    slot_spec = pl.BlockSpec((PEER_SLOTS, t), lambda i, j: (0, i))
    return pl.pallas_call(
        _peer_dense_kernel,
        grid=(n // t, n_exp // ec),
        in_specs=[
            pl.BlockSpec((t, d), lambda i, j: (i, 0)),
            pl.BlockSpec((1, d), lambda i, j: (0, 0)),
            mod_spec, mod_spec, mod_spec,
            slot_spec, slot_spec, slot_spec,
            pl.BlockSpec((ec, d), lambda i, j: (j, 0)),
            pl.BlockSpec((ec, d), lambda i, j: (j, 0)),
        ],
        out_specs=pl.BlockSpec((t, d), lambda i, j: (i, 0)),
        out_shape=jax.ShapeDtypeStruct((n, d), F32),
        scratch_shapes=[
            pltpu.VMEM((PEER_N_KEYS, t, PEER_N_KEYS), BF16),
            pltpu.VMEM((t, d), BF16),
            pltpu.VMEM((t, d), F32),
            pltpu.VMEM((t, PEER_SLOTS), jnp.int32),
            pltpu.VMEM((t, PEER_SLOTS), jnp.int32),
            pltpu.VMEM((t, PEER_SLOTS), F32),
        ],
        compiler_params=_params("arbitrary", "arbitrary"),
        name="peer_dense",
    )(x2, gain, shift, scale, gres, ia, ib, gate, u, v)


def _peer_layer(x2, gain, shift, scale, gres, wqt, sk, u, v, seq):
    ia, ib, gate = _peer_route(x2, gain, shift, scale, wqt, sk, seq)
    return _peer_dense(x2, gain, shift, scale, gres, ia, ib, gate, u, v, seq)


def _kv_proj_kernel(x_ref, gain_ref, shift_ref, scale_ref, wn_ref, wvt_ref, kg_ref,
                    cmp_ref, k_ref, vt_ref):
    h = _norm_mod(x_ref[...], gain_ref[...], shift_ref[0], scale_ref[0]).astype(BF16)
    kvn = _dot(h, wn_ref[...])
    hb = HEAD_BLOCK
    for c in range(4):
        cmp_ref[c] = kvn[:, c * hb:(c + 1) * hb].astype(BF16)
    for c in range(4):
        kk = kvn[:, (4 + c) * hb:(5 + c) * hb]
        kn = kk * lax.rsqrt(jnp.mean(kk * kk, axis=-1, keepdims=True) + EPS) * kg_ref[pl.ds(c // 2, 1), :]
        k_ref[c] = kn.astype(BF16)
    vt = _nt_dot(wvt_ref[...], h)
    for c in range(4):
        for j in range(vt.shape[1] // ATT_TILE):
            vt_ref[c, j] = vt[c * hb:(c + 1) * hb, j * ATT_TILE:(j + 1) * ATT_TILE].astype(BF16)


def _kv_proj(x2, gain, shift, scale, wn, wvt, kg, seq):
    n, d = x2.shape
    tm = ROW_TILE
    tiles_per_batch = seq // tm
    const2 = lambda i: (0, 0)
    mod_spec = pl.BlockSpec((1, 1, d), lambda i: (i // tiles_per_batch, 0, 0))
    return pl.pallas_call(
        _kv_proj_kernel,
        grid=(n // tm,),
        in_specs=[
            pl.BlockSpec((tm, d), lambda i: (i, 0)),
            pl.BlockSpec((1, d), const2),
            mod_spec, mod_spec,
            pl.BlockSpec(wn.shape, const2),
            pl.BlockSpec(wvt.shape, const2),
            pl.BlockSpec(kg.shape, const2),
        ],
        out_specs=[
            pl.BlockSpec((4, tm, HEAD_BLOCK), lambda i: (0, i, 0)),
            pl.BlockSpec((4, tm, HEAD_BLOCK), lambda i: (0, i, 0)),
            pl.BlockSpec((4, tm // ATT_TILE, HEAD_BLOCK, ATT_TILE), lambda i: (0, i, 0, 0)),
        ],
        out_shape=[
            jax.ShapeDtypeStruct((4, n, HEAD_BLOCK), BF16),
            jax.ShapeDtypeStruct((4, n, HEAD_BLOCK), BF16),
            jax.ShapeDtypeStruct((4, n // ATT_TILE, HEAD_BLOCK, ATT_TILE), BF16),
        ],
        compiler_params=_params("arbitrary"),
        name="kv_proj",
    )(x2, gain, shift, scale, wn, wvt, kg)


def _compress_kernel(xk_ref, xv_ref, w1_ref, pos_ref, b1_ref, w2_ref, kg_ref, kc_ref, vct_ref):
    half = w1_ref.shape[1] // 2
    rows = xk_ref.shape[2]
    for kv, x_ref in ((0, xk_ref), (1, xv_ref)):
        x = x_ref[0, 0]
        w1 = w1_ref[kv]
        first = _dot(x, w1[:half])
        second = _dot(x, w1[half:])
        hid = first + pltpu.roll(second, rows - 1, axis=0)
        pos_term = _dot(jnp.broadcast_to(pos_ref[kv], (8, 2 * half)).astype(BF16), w1)[0:1]
        hid = jax.nn.gelu(hid + pos_term + b1_ref[kv])
        out = _dot(hid.astype(BF16), w2_ref[kv])
        if kv == 0:
            out = out * lax.rsqrt(jnp.mean(out * out, axis=-1, keepdims=True) + EPS) * kg_ref[...]
            kc_ref[0, 0] = out.astype(BF16)
        else:
            vct_ref[0, 0] = out.T.astype(BF16)


def _compress(xr, w1, pos, b1, w2, kg, batch):
    _, _, rows, width = xr.shape
    g = NSA_KV_GROUPS
    return pl.pallas_call(
        _compress_kernel,
        grid=(batch, g),
        in_specs=[
            pl.BlockSpec((1, 1, rows, width), lambda b, gi: (gi, b, 0, 0)),
            pl.BlockSpec((1, 1, rows, width), lambda b, gi: (g + gi, b, 0, 0)),
            pl.BlockSpec(w1.shape, lambda b, gi: (0, 0, 0)),
            pl.BlockSpec(pos.shape, lambda b, gi: (0, 0, 0)),
            pl.BlockSpec(b1.shape, lambda b, gi: (0, 0, 0)),
            pl.BlockSpec(w2.shape, lambda b, gi: (0, 0, 0)),
            pl.BlockSpec(kg.shape, lambda b, gi: (0, 0)),
        ],
        out_specs=[
            pl.BlockSpec((1, 1, rows, HEAD_BLOCK), lambda b, gi: (b, gi, 0, 0)),
            pl.BlockSpec((1, 1, HEAD_BLOCK, rows), lambda b, gi: (b, gi, 0, 0)),
        ],
        out_shape=[
            jax.ShapeDtypeStruct((batch, g, rows, HEAD_BLOCK), BF16),
            jax.ShapeDtypeStruct((batch, g, HEAD_BLOCK, rows), BF16),
        ],
        compiler_params=_params("arbitrary", "arbitrary"),
        name="compress",
    )(xr, xr, w1, pos, b1, w2, kg)


def _nsa_q_proj_kernel(x_ref, gain_ref, shift_ref, scale_ref, wq_ref, wgt_ref, bg_ref, qg_ref,
                       q_ref, gt_ref):
    h = _norm_mod(x_ref[...], gain_ref[...], shift_ref[0], scale_ref[0]).astype(BF16)
    q = _dot(h, wq_ref[...])
    hb = HEAD_BLOCK
    for c in range(N_HEADS):
        qq = q[:, c * hb:(c + 1) * hb]
        qn = qq * lax.rsqrt(jnp.mean(qq * qq, axis=-1, keepdims=True) + EPS) * qg_ref[...]
        q_ref[c] = qn.astype(BF16)
    gt_ref[...] = jax.nn.sigmoid(_nt_dot(wgt_ref[...], h) + bg_ref[...])


def _nsa_q_proj(x2, gain, shift, scale, wq, wgt, bg, qg, seq):
    n, d = x2.shape
    tm = ROW_TILE
    tiles_per_batch = seq // tm
    const2 = lambda i: (0, 0)
    mod_spec = pl.BlockSpec((1, 1, d), lambda i: (i // tiles_per_batch, 0, 0))
    rows = wgt.shape[0]
    return pl.pallas_call(
        _nsa_q_proj_kernel,
        grid=(n // tm,),
        in_specs=[
            pl.BlockSpec((tm, d), lambda i: (i, 0)),
            pl.BlockSpec((1, d), const2),
            mod_spec, mod_spec,
            pl.BlockSpec(wq.shape, const2),
            pl.BlockSpec(wgt.shape, const2),
            pl.BlockSpec(bg.shape, const2),
            pl.BlockSpec(qg.shape, const2),
        ],
        out_specs=[
            pl.BlockSpec((N_HEADS, tm, HEAD_BLOCK), lambda i: (0, i, 0)),
            pl.BlockSpec((rows, tm), lambda i: (0, i)),
        ],
        out_shape=[
            jax.ShapeDtypeStruct((N_HEADS, n, HEAD_BLOCK), BF16),
            jax.ShapeDtypeStruct((rows, n), F32),
        ],
        compiler_params=_params("arbitrary"),
        name="nsa_q_proj",
    )(x2, gain, shift, scale, wq, wgt, bg, qg)


def _nsa_attn_kernel(n_slc_bias, n_win_bias, top_n,
                     q_ref, g_ref, kc_ref, vct_ref, ks_ref, vst_ref, kw_ref, vwt_ref,
                     cb_ref, sb_ref, wb_ref, ov_ref, o_ref,
                     sel_ref, oc_ref, m_ref, l_ref, acc_ref, os_ref):
    qi = pl.program_id(2)
    nq = pl.num_programs(2)
    t = ATT_TILE
    hg = NSA_HEADS_PER_GROUP
    q = q_ref[...].reshape(hg * t, HEAD_BLOCK)
    n_cmp_rows = kc_ref.shape[2]
    n_blk = ov_ref.shape[0]

    off = pl.multiple_of((nq - 1 - qi) * (t // CMP_STRIDE), t // CMP_STRIDE)
    bias_c = cb_ref[0, pl.ds(off, n_cmp_rows), :]
    sc = _nt_dot(kc_ref[0, 0], q) + bias_c
    visible = bias_c > 0.5 * NEG_INF
    e = jnp.where(visible, jnp.exp2(sc - jnp.max(sc, axis=0, keepdims=True)), 0.0)
    lsum = jnp.sum(e, axis=0, keepdims=True)
    p = e / jnp.where(lsum > 0.0, lsum, 1.0)
    oc_ref[...] = _dot(vct_ref[0, 0], p.astype(BF16))

    psum = p[:, 0:t]
    for i in range(1, hg):
        psum = psum + p[:, i * t:(i + 1) * t]
    p_hi = psum.astype(BF16)
    p_lo = (psum - p_hi.astype(F32)).astype(BF16)
    imp = _dot(ov_ref[...], p_hi) + _dot(ov_ref[...], p_lo)
    blk = lax.broadcasted_iota(jnp.int32, imp.shape, 0)
    pos = qi * t + lax.broadcasted_iota(jnp.int32, imp.shape, 1)
    back = pos // SEL_BLOCK - blk
    forced = (blk == 0) | ((back >= 0) & (back < SEL_N_LOCAL))
    imp = jnp.where(forced, FORCE, imp)
    imp = jnp.where(back >= 0, imp, NEG_INF)
    sel = jnp.zeros(imp.shape, F32)
    for _ in range(top_n):
        m = jnp.max(imp, axis=0, keepdims=True)
        idx = jnp.min(jnp.where(imp == m, blk, n_blk), axis=0, keepdims=True)
        hit = blk == idx
        sel = jnp.where(hit & (m >= 0.0), 1.0, sel)
        imp = jnp.where(hit, -jnp.inf, imp)
    sel_ref[...] = jnp.where(sel > 0.0, 0.0, NEG_INF)

    blocks_per_tile = t // SEL_BLOCK

    _softmax_init(m_ref, l_ref, acc_ref)

    def slc_body(kt, carry):
        k = ks_ref[0, pl.ds(pl.multiple_of(kt * t, t), t), :]
        bias = sb_ref[0, jnp.minimum(qi - kt, n_slc_bias - 1)]
        rows = [jnp.broadcast_to(sel_ref[pl.ds(kt * blocks_per_tile + jj, 1), :], (SEL_BLOCK, t))
                for jj in range(blocks_per_tile)]
        mask = jnp.concatenate(rows, axis=0)
        s = _nt_dot(k, q) + bias + jnp.concatenate([mask] * hg, axis=1)
        _online_softmax_step(s, vst_ref[0, kt], m_ref, l_ref, acc_ref)
        return carry

    lax.fori_loop(0, qi + 1, slc_body, 0)
    os_ref[...] = acc_ref[...] / l_ref[...]

    _softmax_init(m_ref, l_ref, acc_ref)

    def win_body(kt, carry):
        k = kw_ref[0, pl.ds(pl.multiple_of(kt * t, t), t), :]
        s = _nt_dot(k, q) + wb_ref[0, qi - kt]
        _online_softmax_step(s, vwt_ref[0, kt], m_ref, l_ref, acc_ref)
        return carry

    lax.fori_loop(jnp.maximum(qi - (n_win_bias - 1), 0), qi + 1, win_body, 0)
    ow = acc_ref[...] / l_ref[...]

    gates = g_ref[...]
    for i in range(hg):
        sl = slice(i * t, (i + 1) * t)
        o = (gates[3 * i:3 * i + 1] * oc_ref[:, sl] + gates[3 * i + 1:3 * i + 2] * os_ref[:, sl]
             + gates[3 * i + 2:3 * i + 3] * ow[:, sl])
        o_ref[:, i * HEAD_BLOCK:(i + 1) * HEAD_BLOCK] = o.T.astype(BF16)


def _nsa_attn(q, gt, kc, vct, k, vt, cmp_bias, slc_bias, win_bias, ovt, batch, seq):
    t = ATT_TILE
    nq = seq // t
    hg = NSA_HEADS_PER_GROUP
    g = NSA_KV_GROUPS
    n = batch * seq
    top_n = min(SEL_TOPN, seq // SEL_BLOCK)
    lanes = hg * t
    idx3 = lambda b, gi, i: (gi, 0, 0)
    idx4 = lambda b, gi, i: (gi, 0, 0, 0)
    return pl.pallas_call(
        functools.partial(_nsa_attn_kernel, slc_bias.shape[1], win_bias.shape[1], top_n),
        grid=(batch, g, nq),
        in_specs=[
            pl.BlockSpec((hg, t, HEAD_BLOCK), lambda b, gi, i: (gi, b * nq + i, 0)),
            pl.BlockSpec((GATE_ROWS, t), lambda b, gi, i: (gi, b * nq + i)),
            pl.BlockSpec((1, 1) + kc.shape[2:], lambda b, gi, i: (b, gi, 0, 0)),
            pl.BlockSpec((1, 1) + vct.shape[2:], lambda b, gi, i: (b, gi, 0, 0)),
            pl.BlockSpec((1, seq, HEAD_BLOCK), lambda b, gi, i: (gi, b, 0)),
            pl.BlockSpec((1, nq, HEAD_BLOCK, t), lambda b, gi, i: (gi, b, 0, 0)),
            pl.BlockSpec((1, seq, HEAD_BLOCK), lambda b, gi, i: (g + gi, b, 0)),
            pl.BlockSpec((1, nq, HEAD_BLOCK, t), lambda b, gi, i: (g + gi, b, 0, 0)),
            pl.BlockSpec((1,) + cmp_bias.shape[1:], idx3),
            pl.BlockSpec((1,) + slc_bias.shape[1:], idx4),
            pl.BlockSpec((1,) + win_bias.shape[1:], idx4),
            pl.BlockSpec(ovt.shape, lambda b, gi, i: (0, 0)),
        ],
        out_specs=pl.BlockSpec((t, hg * HEAD_BLOCK), lambda b, gi, i: (b * nq + i, gi)),
        out_shape=jax.ShapeDtypeStruct((n, N_HEADS * HEAD_BLOCK), BF16),
        scratch_shapes=[
            pltpu.VMEM((ovt.shape[0], t), F32),
            pltpu.VMEM((HEAD_BLOCK, lanes), F32),
            pltpu.VMEM((1, lanes), F32),
            pltpu.VMEM((1, lanes), F32),
            pltpu.VMEM((HEAD_BLOCK, lanes), F32),
            pltpu.VMEM((HEAD_BLOCK, lanes), F32),
        ],
        compiler_params=_params("arbitrary", "arbitrary", "arbitrary"),
        name="nsa_attn",
    )(q, gt, kc, vct, k, vt, k, vt, cmp_bias, slc_bias, win_bias, ovt)


def _toeplitz_kernel(t, n_tiles, v_ref, o_ref):
    sub = 8
    x = jnp.broadcast_to(v_ref[0], (sub, v_ref.shape[2]))
    for g in range(t // sub):
        shifted = pltpu.roll(x, sub * g, 1, stride=1, stride_axis=0)
        for d in range(n_tiles):
            o_ref[0, d, sub * g:sub * (g + 1), :] = shifted[:, (d + 1) * t:(d + 2) * t]


def _bias_tiles(profile, t, n_tiles, window=None):
    heads = profile.shape[0]
    span = n_tiles * t
    vis = profile[:, :span]
    if window is not None:
        vis = jnp.where(jnp.arange(span)[None] < window, vis, NEG_INF)
    v = jnp.concatenate([jnp.full((heads, t), NEG_INF, F32), vis], axis=1)
    length = span + t
    return pl.pallas_call(
        functools.partial(_toeplitz_kernel, t, n_tiles),
        grid=(heads,),
        in_specs=[pl.BlockSpec((1, 1, length), lambda h: (h, 0, 0))],
        out_specs=pl.BlockSpec((1, n_tiles, t, t), lambda h: (h, 0, 0, 0)),
        out_shape=jax.ShapeDtypeStruct((heads, n_tiles, t, t), F32),
        compiler_params=_params("arbitrary"),
        name="bias_tiles",
    )(v.reshape(heads, 1, length))


def _cmp_bias_strip(profile, seq):
    t = ATT_TILE
    heads = profile.shape[0]
    per = t // CMP_STRIDE
    n_rho = 2 * (seq // t) - 1
    c = seq - t - (CMP_LEN - 1)
    pad = t * (n_rho - 1) + CMP_STRIDE * (per - 1) - c
    g = jnp.concatenate([jnp.full((heads, pad), NEG_INF, F32), profile], axis=1)
    slabs = []
    for sigma in range(per):
        base = c - CMP_STRIDE * sigma + pad
        seg = g[:, base - t * (n_rho - 1):base + t].reshape(heads, n_rho, t)
        slabs.append(seg[:, ::-1])
    return jnp.stack(slabs, axis=2).reshape(heads, n_rho * per, t)


def _n_far_tiles(t):
    return -(-(REL_MAX_DIST + t - 1) // t) + 1


def _group_lanes(tiles):
    hg = NSA_HEADS_PER_GROUP
    parts = tiles.reshape((NSA_KV_GROUPS, hg) + tiles.shape[1:])
    return jnp.concatenate([parts[:, i] for i in range(hg)], axis=-1)


def kernel(x, c, rel_bias, ada_w, ada_b, norm_mix, norm_ffn, diff_w_in, diff_w_out, diff_lambda,
           diff_q_gain, diff_k_gain, diff_subln, kv_norm, kv_ada_w, kv_ada_b, kv_w, cmp_pos, cmp_w1,
           cmp_b1, cmp_w2, nsa_k_gain, nsa_w_in, nsa_b_gate, nsa_w_out, nsa_q_gain, peer_w_q,
           peer_subkeys, peer_u, peer_v):
    batch, seq, d = x.shape
    n = batch * seq
    t = ATT_TILE
    nq = seq // t
    width = N_HEADS * HEAD_BLOCK
    x2 = x.reshape(n, d)

    c_pad = jnp.pad(c, ((0, 8 - batch % 8 if batch % 8 else 0), (0, 0)))
    mods = _ada(c_pad, ada_w, ada_b, 1536)
    kv_mods = _ada(c_pad, kv_ada_w[None], kv_ada_b[None], 1024)

    def mod(arr, layer, k):
        return arr[layer, :batch, k * d:(k + 1) * d].reshape(batch, 1, d)

    profile = rel_bias[_rel_bucket(jnp.arange(seq))].T * LOG2_E

    lam_init = 0.8 - 0.6 * math.exp(-0.3 * 0)
    w_in = diff_w_in[0]
    wqk = w_in[:, :2 * width].astype(BF16)
    wvt = w_in[:, 2 * width:].T.astype(BF16)
    grp = jnp.asarray(np.kron(np.eye(width // DIFF_HEAD_DIM), np.ones((DIFF_HEAD_DIM, DIFF_HEAD_DIM))), BF16)
    reps = width // DIFF_HEAD_DIM
    qg = (jnp.tile(diff_q_gain[0], reps) * (DIFF_HEAD_DIM ** -0.5 * LOG2_E)).reshape(1, width)
    kg = jnp.tile(diff_k_gain[0], reps).reshape(1, width)
    q, k, vt = _diff_proj(x2, norm_mix[0:1], mod(mods, 0, 0), mod(mods, 0, 1), wqk, wvt, grp, qg, kg, seq)
    diff_bias = _bias_tiles(profile, DIFF_TILE, min(_n_far_tiles(DIFF_TILE), seq // DIFF_TILE))
    sub_col = (diff_subln[0] * (1.0 - lam_init)).reshape(HEAD_BLOCK, 1)
    o = _diff_attn(q, k, vt, diff_bias, diff_lambda[0], sub_col, batch, seq, lam_init)
    x2 = _out_proj(x2, o, diff_w_out[0].astype(BF16), mod(mods, 0, 2), seq)

    def peer(x2, layer):
        return _peer_layer(x2, norm_ffn[layer:layer + 1], mod(mods, layer, 3), mod(mods, layer, 4),
                           mod(mods, layer, 5), peer_w_q[layer].T.astype(BF16),
                           peer_subkeys[layer].astype(BF16), peer_u[layer].astype(BF16),
                           peer_v[layer].astype(BF16), seq)

    x2 = peer(x2, 0)

    hb = HEAD_BLOCK

    def kv_cols(branch, kv, g):
        start = branch * 4 * hb + kv * 2 * hb + g * hb
        return kv_w[:, start:start + hb]

    wn = jnp.concatenate([kv_cols(0, 0, 0), kv_cols(0, 0, 1), kv_cols(0, 1, 0), kv_cols(0, 1, 1),
                          kv_cols(1, 0, 0), kv_cols(1, 0, 1), kv_cols(2, 0, 0), kv_cols(2, 0, 1)],
                         axis=1).astype(BF16)
    wvt_kv = jnp.concatenate([kv_cols(1, 1, 0), kv_cols(1, 1, 1), kv_cols(2, 1, 0), kv_cols(2, 1, 1)],
                             axis=1).T.astype(BF16)
    cmp_raw, k_sw, vt_sw = _kv_proj(x2, kv_norm.reshape(1, d), mod(kv_mods, 0, 0), mod(kv_mods, 0, 1),
                                    wn, wvt_kv, nsa_k_gain[1:3], seq)
    xr = cmp_raw.reshape(4, batch, seq // CMP_STRIDE, CMP_STRIDE * hb)
    kc, vct = _compress(xr, cmp_w1.astype(BF16), cmp_pos.reshape(2, 1, CMP_LEN * hb),
                        cmp_b1.reshape(2, 1, CMP_HIDDEN), cmp_w2.astype(BF16), nsa_k_gain[0:1], batch)

    hg = NSA_HEADS_PER_GROUP
    w_in1 = nsa_w_in[0]
    wq = w_in1[:, :width].astype(BF16)
    w_gate = w_in1[:, width:].reshape(d, NSA_KV_GROUPS, hg * N_BRANCH)
    w_gate = jnp.pad(w_gate, ((0, 0), (0, 0), (0, GATE_ROWS - hg * N_BRANCH)))
    wgt = w_gate.reshape(d, NSA_KV_GROUPS * GATE_ROWS).T.astype(BF16)
    b_gate = jnp.pad(nsa_b_gate[0].reshape(NSA_KV_GROUPS, hg * N_BRANCH),
                     ((0, 0), (0, GATE_ROWS - hg * N_BRANCH))).reshape(NSA_KV_GROUPS * GATE_ROWS, 1)
    qg1 = (nsa_q_gain[0] * (hb ** -0.5 * LOG2_E)).reshape(1, hb)
    q1, gt = _nsa_q_proj(x2, norm_mix[1:2], mod(mods, 1, 0), mod(mods, 1, 1), wq, wgt, b_gate, qg1, seq)

    slc_bias = _group_lanes(_bias_tiles(profile, t, min(_n_far_tiles(t), nq)))
    win_bias = _group_lanes(_bias_tiles(profile, t, min(WINDOW // t + 1, nq), window=WINDOW))
    cmp_bias = _group_lanes(_cmp_bias_strip(profile, seq))
    n_rows = seq // CMP_STRIDE
    n_sel = seq // SEL_BLOCK
    cmp_start = np.arange(n_rows) * CMP_STRIDE
    sel_start = np.arange(n_sel) * SEL_BLOCK
    overlap = np.clip(np.minimum(cmp_start[:, None] + CMP_LEN, sel_start[None, :] + SEL_BLOCK)
                      - np.maximum(cmp_start[:, None], sel_start[None, :]), 0, None) / CMP_LEN
    overlap[(seq - CMP_LEN) // CMP_STRIDE + 1:] = 0.0
    ovt = jnp.asarray(overlap.T, BF16)
    o1 = _nsa_attn(q1, gt, kc, vct, k_sw, vt_sw, cmp_bias, slc_bias, win_bias, ovt, batch, seq)
    x2 = _out_proj(x2, o1, nsa_w_out[0].astype(BF16), mod(mods, 1, 2), seq)

    x2 = peer(x2, 1)
    return x2.reshape(batch, seq, d)
```

```python
import functools
import math

import jax
import jax.numpy as jnp
import numpy as np
from jax import lax
from jax.experimental import pallas as pl
from jax.experimental.pallas import tpu as pltpu

F32 = jnp.float32
BF16 = jnp.bfloat16

N_HEADS = 8
N_ADA = 6
EPS = 1e-6
NEG_INF = -1e30
FORCE = 1e9
LOG2_E = math.log2(math.e)

DIFF_HEAD_DIM = 64
HEAD_BLOCK = 128

NSA_KV_GROUPS = 2
NSA_HEADS_PER_GROUP = N_HEADS // NSA_KV_GROUPS
N_BRANCH = 3
CMP_LEN = 32
CMP_STRIDE = 16
CMP_HIDDEN = 256
SEL_BLOCK = 64
SEL_TOPN = 16
SEL_N_LOCAL = 2
WINDOW = 512
GATE_ROWS = 16

REL_BUCKETS = 32
REL_MAX_DIST = 1024

PEER_HEADS = 8
PEER_N_KEYS = 128
PEER_TOPK = 16
PEER_SLOTS = PEER_HEADS * PEER_TOPK

ROW_TILE = 512
ATT_TILE = 256
DIFF_TILE = 512
ROUTE_TILE = 512
PEER_TILE = 512
PEER_BUILD_GROUP = 32
PEER_A_CHUNK = 8
VMEM_LIMIT = 56 * 1024 * 1024

NT_DIMS = (((1,), (1,)), ((), ()))


def _params(*sem):
    return pltpu.CompilerParams(dimension_semantics=sem, vmem_limit_bytes=VMEM_LIMIT)


def _nt_dot(a, b):
    return lax.dot_general(a, b, NT_DIMS, preferred_element_type=F32)


def _dot(a, b):
    return jnp.dot(a, b, preferred_element_type=F32)


def _rel_bucket(dist):
    max_exact = REL_BUCKETS // 2
    n = jnp.maximum(dist, 0)
    nf = jnp.maximum(n, 1).astype(jnp.float32)
    large = max_exact + (jnp.log(nf / max_exact) / math.log(REL_MAX_DIST / max_exact)
                         * (REL_BUCKETS - max_exact)).astype(jnp.int32)
    large = jnp.minimum(large, REL_BUCKETS - 1)
    return jnp.where(n < max_exact, n, large)


def _norm_mod(x, gain, shift, scale):
    y = x * lax.rsqrt(jnp.mean(x * x, axis=-1, keepdims=True) + EPS) * gain
    return y * (1.0 + scale) + shift


def _ada_kernel(c_ref, w_ref, b_ref, o_ref):
    c = c_ref[...]
    ca = c * jax.nn.sigmoid(c)
    o_ref[0] = jnp.dot(ca, w_ref[0], preferred_element_type=F32,
                       precision=lax.Precision.HIGHEST) + b_ref[0]


def _ada(c_pad, w, b, col_tile):
    layers, d, n = w.shape
    rows = c_pad.shape[0]
    return pl.pallas_call(
        _ada_kernel,
        grid=(layers, n // col_tile),
        in_specs=[
            pl.BlockSpec((rows, d), lambda l, j: (0, 0)),
            pl.BlockSpec((1, d, col_tile), lambda l, j: (l, 0, j)),
            pl.BlockSpec((1, 1, col_tile), lambda l, j: (l, 0, j)),
        ],
        out_specs=pl.BlockSpec((1, rows, col_tile), lambda l, j: (l, 0, j)),
        out_shape=jax.ShapeDtypeStruct((layers, rows, n), F32),
        compiler_params=_params("arbitrary", "arbitrary"),
        name="ada",
    )(c_pad, w, b.reshape(layers, 1, n))


def _diff_proj_kernel(x_ref, gain_ref, shift_ref, scale_ref, wqk_ref, wvt_ref, grp_ref,
                      qg_ref, kg_ref, q_ref, k_ref, vt_ref):
    h = _norm_mod(x_ref[...], gain_ref[...], shift_ref[0], scale_ref[0]).astype(BF16)
    qk = _dot(h, wqk_ref[...])
    width = N_HEADS * HEAD_BLOCK
    for part, out_ref, g_ref in ((0, q_ref, qg_ref), (1, k_ref, kg_ref)):
        z = qk[:, part * width:(part + 1) * width]
        zz = z * z
        hi = zz.astype(BF16)
        lo = (zz - hi.astype(F32)).astype(BF16)
        ss = _dot(hi, grp_ref[...]) + _dot(lo, grp_ref[...])
        zn = z * lax.rsqrt(ss * (1.0 / DIFF_HEAD_DIM) + EPS) * g_ref[...]
        for c in range(N_HEADS):
            out_ref[c] = zn[:, c * HEAD_BLOCK:(c + 1) * HEAD_BLOCK].astype(BF16)
    vt = _nt_dot(wvt_ref[...], h)
    n_chunk = vt.shape[1] // DIFF_TILE
    for c in range(N_HEADS):
        for j in range(n_chunk):
            vt_ref[c, j] = vt[c * HEAD_BLOCK:(c + 1) * HEAD_BLOCK,
                              j * DIFF_TILE:(j + 1) * DIFF_TILE].astype(BF16)


def _diff_proj(x2, gain, shift, scale, wqk, wvt, grp, qg, kg, seq):
    n, d = x2.shape
    tm = ROW_TILE
    tiles_per_batch = seq // tm
    width = N_HEADS * HEAD_BLOCK
    const2 = lambda i: (0, 0)
    mod_spec = pl.BlockSpec((1, 1, d), lambda i: (i // tiles_per_batch, 0, 0))
    return pl.pallas_call(
        _diff_proj_kernel,
        grid=(n // tm,),
        in_specs=[
            pl.BlockSpec((tm, d), lambda i: (i, 0)),
            pl.BlockSpec((1, d), const2),
            mod_spec, mod_spec,
            pl.BlockSpec((d, 2 * width), const2),
            pl.BlockSpec((width, d), const2),
            pl.BlockSpec((width, width), const2),
            pl.BlockSpec((1, width), const2),
            pl.BlockSpec((1, width), const2),
        ],
        out_specs=[
            pl.BlockSpec((N_HEADS, tm, HEAD_BLOCK), lambda i: (0, i, 0)),
            pl.BlockSpec((N_HEADS, tm, HEAD_BLOCK), lambda i: (0, i, 0)),
            pl.BlockSpec((N_HEADS, tm // DIFF_TILE, HEAD_BLOCK, DIFF_TILE), lambda i: (0, i, 0, 0)),
        ],
        out_shape=[
            jax.ShapeDtypeStruct((N_HEADS, n, HEAD_BLOCK), BF16),
            jax.ShapeDtypeStruct((N_HEADS, n, HEAD_BLOCK), BF16),
            jax.ShapeDtypeStruct((N_HEADS, n // DIFF_TILE, HEAD_BLOCK, DIFF_TILE), BF16),
        ],
        compiler_params=_params("arbitrary"),
        name="diff_proj",
    )(x2, gain, shift, scale, wqk, wvt, grp, qg, kg)


def _online_softmax_step(s, vt, m_ref, l_ref, acc_ref):
    m_old = m_ref[...]
    m_new = jnp.maximum(m_old, jnp.max(s, axis=0, keepdims=True))
    alpha = jnp.exp2(m_old - m_new)
    p = jnp.exp2(s - m_new)
    l_ref[...] = alpha * l_ref[...] + jnp.sum(p, axis=0, keepdims=True)
    acc_ref[...] = alpha * acc_ref[...] + _dot(vt, p.astype(BF16))
    m_ref[...] = m_new


def _softmax_init(m_ref, l_ref, acc_ref):
    m_ref[...] = jnp.full(m_ref.shape, NEG_INF, F32)
    l_ref[...] = jnp.zeros(l_ref.shape, F32)
    acc_ref[...] = jnp.zeros(acc_ref.shape, F32)


def _diff_attn_kernel(lam_init, n_bias, q_ref, k_ref, vt_ref, bias_ref, lam_ref, sub_ref,
                      o_ref, m_ref, l_ref, acc_ref):
    qi = pl.program_id(2)
    t = DIFF_TILE
    q = q_ref[0]
    lane = lax.broadcasted_iota(jnp.int32, q.shape, 1)
    zero = jnp.zeros_like(q)
    qs = jnp.concatenate([jnp.where(lane < DIFF_HEAD_DIM, q, zero),
                          jnp.where(lane >= DIFF_HEAD_DIM, q, zero)], axis=0)
    _softmax_init(m_ref, l_ref, acc_ref)

    def body(kt, carry):
        k = k_ref[0, pl.ds(pl.multiple_of(kt * t, t), t), :]
        bias = bias_ref[0, jnp.minimum(qi - kt, n_bias - 1)]
        s = _nt_dot(k, qs) + jnp.concatenate([bias, bias], axis=1)
        _online_softmax_step(s, vt_ref[0, kt], m_ref, l_ref, acc_ref)
        return carry

    lax.fori_loop(0, qi + 1, body, 0)

    lam = lam_ref[...]
    lam_full = (jnp.exp(jnp.sum(lam[0:1] * lam[1:2], axis=1, keepdims=True))
                - jnp.exp(jnp.sum(lam[2:3] * lam[3:4], axis=1, keepdims=True)) + lam_init)
    o = acc_ref[...] / l_ref[...]
    o = o[:, :t] - lam_full * o[:, t:]
    y = o * lax.rsqrt(jnp.mean(o * o, axis=0, keepdims=True) + EPS) * sub_ref[...]
    o_ref[...] = y.T.astype(BF16)


def _diff_attn(q, k, vt, bias, lam, sub_col, batch, seq, lam_init):
    t = DIFF_TILE
    nq = seq // t
    n_bias = bias.shape[1]
    n = batch * seq
    return pl.pallas_call(
        functools.partial(_diff_attn_kernel, lam_init, n_bias),
        grid=(batch, N_HEADS, nq),
        in_specs=[
            pl.BlockSpec((1, t, HEAD_BLOCK), lambda b, h, i: (h, b * nq + i, 0)),
            pl.BlockSpec((1, seq, HEAD_BLOCK), lambda b, h, i: (h, b, 0)),
            pl.BlockSpec((1, nq, HEAD_BLOCK, t), lambda b, h, i: (h, b, 0, 0)),
            pl.BlockSpec((1, n_bias, t, t), lambda b, h, i: (h, 0, 0, 0)),
            pl.BlockSpec(lam.shape, lambda b, h, i: (0, 0)),
            pl.BlockSpec(sub_col.shape, lambda b, h, i: (0, 0)),
        ],
        out_specs=pl.BlockSpec((t, HEAD_BLOCK), lambda b, h, i: (b * nq + i, h)),
        out_shape=jax.ShapeDtypeStruct((n, N_HEADS * HEAD_BLOCK), BF16),
        scratch_shapes=[
            pltpu.VMEM((1, 2 * t), F32),
            pltpu.VMEM((1, 2 * t), F32),
            pltpu.VMEM((HEAD_BLOCK, 2 * t), F32),
        ],
        compiler_params=_params("arbitrary", "arbitrary", "arbitrary"),
        name="diff_attn",
    )(q, k, vt, bias, lam, sub_col)


def _out_proj_kernel(x_ref, o_ref, w_ref, g_ref, y_ref):
    y_ref[...] = x_ref[...] + g_ref[0] * _dot(o_ref[...], w_ref[...])


def _out_proj(x2, o, w, gate, seq):
    n, d = x2.shape
    tm = ROW_TILE
    tiles_per_batch = seq // tm
    return pl.pallas_call(
        _out_proj_kernel,
        grid=(n // tm,),
        in_specs=[
            pl.BlockSpec((tm, d), lambda i: (i, 0)),
            pl.BlockSpec((tm, o.shape[1]), lambda i: (i, 0)),
            pl.BlockSpec(w.shape, lambda i: (0, 0)),
            pl.BlockSpec((1, 1, d), lambda i: (i // tiles_per_batch, 0, 0)),
        ],
        out_specs=pl.BlockSpec((tm, d), lambda i: (i, 0)),
        out_shape=jax.ShapeDtypeStruct((n, d), F32),
        compiler_params=_params("arbitrary"),
        name="out_proj",
    )(x2, o, w, gate)


def _top_rows(s, count, val_ref, idx_ref, extra=()):
    rows = s.shape[0]
    iota = lax.broadcasted_iota(jnp.int32, s.shape, 0)
    for r in range(count):
        m = jnp.max(s, axis=0, keepdims=True)
        idx = jnp.min(jnp.where(s == m, iota, rows), axis=0, keepdims=True)
        hit = iota == idx
        val_ref[pl.ds(r, 1), :] = m
        if idx_ref is not None:
            idx_ref[pl.ds(r, 1), :] = idx
        for table, ref in extra:
            ref[pl.ds(r, 1), :] = jnp.max(jnp.where(hit, table, -1), axis=0, keepdims=True)
        s = jnp.where(hit, -jnp.inf, s)


def _peer_route_kernel(x_ref, gain_ref, shift_ref, scale_ref, wqt_ref, sk_ref,
                       ia_ref, ib_ref, gate_ref,
                       qt_ref, v1_ref, i1_ref, v2_ref, i2_ref, tv_ref, ta_ref, tb_ref):
    h = _norm_mod(x_ref[...], gain_ref[...], shift_ref[0], scale_ref[0]).astype(BF16)
    qt_ref[...] = _nt_dot(wqt_ref[...], h)
    k = PEER_TOPK
    half = PEER_N_KEYS

    def head(hd, carry):
        base = pl.multiple_of(hd * 2 * half, 2 * half)
        s1 = _dot(sk_ref[0], qt_ref[pl.ds(base, half), :].astype(BF16))
        s2 = _dot(sk_ref[1], qt_ref[pl.ds(base + half, half), :].astype(BF16))
        _top_rows(s1, k, v1_ref, i1_ref)
        _top_rows(s2, k, v2_ref, i2_ref)
        v1, i1, v2, i2 = v1_ref[...], i1_ref[...], v2_ref[...], i2_ref[...]
        tok = v1.shape[1]
        spans = [(0, k)] + [(a, k // 2) for a in range(1, k // 2)]
        cand = [v1[a:a + 1] + v2[:nb] for a, nb in spans] + [v1[k // 2:] + v2[0:1]]
        cand_a = ([jnp.broadcast_to(i1[a:a + 1], (nb, tok)) for a, nb in spans] + [i1[k // 2:]])
        cand_b = [i2[:nb] for a, nb in spans] + [jnp.broadcast_to(i2[0:1], (k - k // 2, tok))]
        _top_rows(jnp.concatenate(cand, axis=0), k, tv_ref, None,
                  extra=((jnp.concatenate(cand_a, axis=0), ta_ref), (jnp.concatenate(cand_b, axis=0), tb_ref)))
        top = tv_ref[...]
        e = jnp.exp(top - jnp.max(top, axis=0, keepdims=True))
        gate = e / jnp.sum(e, axis=0, keepdims=True)
        row = pl.multiple_of(hd * k, k)
        ia_ref[pl.ds(row, k), :] = ta_ref[...]
        ib_ref[pl.ds(row, k), :] = tb_ref[...]
        gate_ref[pl.ds(row, k), :] = gate
        return carry

    lax.fori_loop(0, PEER_HEADS, head, 0)


def _peer_route(x2, gain, shift, scale, wqt, sk, seq):
    n, d = x2.shape
    t = ROUTE_TILE
    tiles_per_batch = seq // t
    const2 = lambda i: (0, 0)
    mod_spec = pl.BlockSpec((1, 1, d), lambda i: (i // tiles_per_batch, 0, 0))
    slot_spec = pl.BlockSpec((PEER_SLOTS, t), lambda i: (0, i))
    k = PEER_TOPK
    return pl.pallas_call(
        _peer_route_kernel,
        grid=(n // t,),
        in_specs=[
            pl.BlockSpec((t, d), lambda i: (i, 0)),
            pl.BlockSpec((1, d), const2),
            mod_spec, mod_spec,
            pl.BlockSpec(wqt.shape, const2),
            pl.BlockSpec(sk.shape, lambda i: (0, 0, 0)),
        ],
        out_specs=[slot_spec, slot_spec, slot_spec],
        out_shape=[
            jax.ShapeDtypeStruct((PEER_SLOTS, n), jnp.int32),
            jax.ShapeDtypeStruct((PEER_SLOTS, n), jnp.int32),
            jax.ShapeDtypeStruct((PEER_SLOTS, n), F32),
        ],
        scratch_shapes=[
            pltpu.VMEM((wqt.shape[0], t), F32),
            pltpu.VMEM((k, t), F32), pltpu.VMEM((k, t), jnp.int32),
            pltpu.VMEM((k, t), F32), pltpu.VMEM((k, t), jnp.int32),
            pltpu.VMEM((k, t), F32), pltpu.VMEM((k, t), jnp.int32), pltpu.VMEM((k, t), jnp.int32),
        ],
        compiler_params=_params("arbitrary"),
        name="peer_route",
    )(x2, gain, shift, scale, wqt, sk)


def _peer_dense_kernel(x_ref, gain_ref, shift_ref, scale_ref, gres_ref, ia_ref, ib_ref, gate_ref,
                       u_ref, v_ref, y_ref, w_ref, h_ref, acc_ref, iat_ref, ibt_ref, gt_ref):
    j = pl.program_id(1)
    t = x_ref.shape[0]
    nk = PEER_N_KEYS

    @pl.when(j == 0)
    def _():
        h_ref[...] = _norm_mod(x_ref[...], gain_ref[...], shift_ref[0], scale_ref[0]).astype(BF16)
        acc_ref[...] = jnp.zeros(acc_ref.shape, F32)
        iat_ref[...] = ia_ref[...].T
        ibt_ref[...] = ib_ref[...].T
        gt_ref[...] = gate_ref[...].T
        key = lax.broadcasted_iota(jnp.int32, (nk, PEER_SLOTS), 0)
        sub = 8

        def build(grp, carry):
            base = pl.multiple_of(grp * PEER_BUILD_GROUP, PEER_BUILD_GROUP)
            slabs = []
            for s0 in range(0, PEER_BUILD_GROUP, sub):
                grids = []
                for i in range(sub):
                    row = pl.ds(base + s0 + i, 1)
                    a_hot = jnp.where(key == iat_ref[row, :], 1.0, 0.0).astype(BF16)
                    b_gate = jnp.where(key == ibt_ref[row, :], gt_ref[row, :], 0.0).astype(BF16)
                    grids.append(_nt_dot(a_hot, b_gate))
                slabs.append(pltpu.einshape("tab->atb", jnp.stack(grids, axis=0)))
            w_ref[:, pl.ds(base, PEER_BUILD_GROUP), :] = jnp.concatenate(slabs, axis=1).astype(BF16)
            return carry

        lax.fori_loop(0, t // PEER_BUILD_GROUP, build, 0)

    act = _nt_dot(h_ref[...], u_ref[...])
    parts = []
    for al in range(PEER_A_CHUNK):
        w_a = w_ref[j * PEER_A_CHUNK + al]
        parts.append(w_a * jax.nn.gelu(act[:, al * nk:(al + 1) * nk]).astype(BF16))
    acc_ref[...] += _dot(jnp.concatenate(parts, axis=1), v_ref[...])

    @pl.when(j == pl.num_programs(1) - 1)
    def _():
        y_ref[...] = x_ref[...] + gres_ref[0] * acc_ref[...]


def _peer_dense(x2, gain, shift, scale, gres, ia, ib, gate, u, v, seq):
    n, d = x2.shape
    t = PEER_TILE
    tiles_per_batch = seq // t
    ec = PEER_A_CHUNK * PEER_N_KEYS
    n_exp = u.shape[0]
    mod_spec = pl.BlockSpec((1, 1, d), lambda i, j: (i // tiles_per_batch, 0, 0))
    slot_spec = pl.BlockSpec((PEER_SLOTS, t), lambda i, j: (0, i))
    return pl.pallas_call(
        _peer_dense_kernel,
        grid=(n // t, n_exp // ec),
        in_specs=[
            pl.BlockSpec((t, d), lambda i, j: (i, 0)),
            pl.BlockSpec((1, d), lambda i, j: (0, 0)),
            mod_spec, mod_spec, mod_spec,
            slot_spec, slot_spec, slot_spec,
            pl.BlockSpec((ec, d), lambda i, j: (j, 0)),
            pl.BlockSpec((ec, d), lambda i, j: (j, 0)),
        ],
        out_specs=pl.BlockSpec((t, d), lambda i, j: (i, 0)),
        out_shape=jax.ShapeDtypeStruct((n, d), F32),
        scratch_shapes=[
            pltpu.VMEM((PEER_N_KEYS, t, PEER_N_KEYS), BF16),
            pltpu.VMEM((t, d), BF16),
            pltpu.VMEM((t, d), F32),
            pltpu.VMEM((t, PEER_SLOTS), jnp.int32),
            pltpu.VMEM((t, PEER_SLOTS), jnp.int32),
            pltpu.VMEM((t, PEER_SLOTS), F32),
        ],
        compiler_params=_params("arbitrary", "arbitrary"),
        name="peer_dense",
    )(x2, gain, shift, scale, gres, ia, ib, gate, u, v)


def _peer_layer(x2, gain, shift, scale, gres, wqt, sk, u, v, seq):
    ia, ib, gate = _peer_route(x2, gain, shift, scale, wqt, sk, seq)
    return _peer_dense(x2, gain, shift, scale, gres, ia, ib, gate, u, v, seq)


def _kv_proj_kernel(x_ref, gain_ref, shift_ref, scale_ref, wn_ref, wvt_ref, kg_ref,
                    cmp_ref, k_ref, vt_ref):
    h = _norm_mod(x_ref[...], gain_ref[...], shift_ref[0], scale_ref[0]).astype(BF16)
    kvn = _dot(h, wn_ref[...])
    hb = HEAD_BLOCK
    for c in range(4):
        cmp_ref[c] = kvn[:, c * hb:(c + 1) * hb].astype(BF16)
    for c in range(4):
        kk = kvn[:, (4 + c) * hb:(5 + c) * hb]
        kn = kk * lax.rsqrt(jnp.mean(kk * kk, axis=-1, keepdims=True) + EPS) * kg_ref[pl.ds(c // 2, 1), :]
        k_ref[c] = kn.astype(BF16)
    vt = _nt_dot(wvt_ref[...], h)
    for c in range(4):
        for j in range(vt.shape[1] // ATT_TILE):
            vt_ref[c, j] = vt[c * hb:(c + 1) * hb, j * ATT_TILE:(j + 1) * ATT_TILE].astype(BF16)


def _kv_proj(x2, gain, shift, scale, wn, wvt, kg, seq):
    n, d = x2.shape
    tm = ROW_TILE
    tiles_per_batch = seq // tm
    const2 = lambda i: (0, 0)
    mod_spec = pl.BlockSpec((1, 1, d), lambda i: (i // tiles_per_batch, 0, 0))
    return pl.pallas_call(
        _kv_proj_kernel,
        grid=(n // tm,),
        in_specs=[
            pl.BlockSpec((tm, d), lambda i: (i, 0)),
            pl.BlockSpec((1, d), const2),
            mod_spec, mod_spec,
            pl.BlockSpec(wn.shape, const2),
            pl.BlockSpec(wvt.shape, const2),
            pl.BlockSpec(kg.shape, const2),
        ],
        out_specs=[
            pl.BlockSpec((4, tm, HEAD_BLOCK), lambda i: (0, i, 0)),
            pl.BlockSpec((4, tm, HEAD_BLOCK), lambda i: (0, i, 0)),
            pl.BlockSpec((4, tm // ATT_TILE, HEAD_BLOCK, ATT_TILE), lambda i: (0, i, 0, 0)),
        ],
        out_shape=[
            jax.ShapeDtypeStruct((4, n, HEAD_BLOCK), BF16),
            jax.ShapeDtypeStruct((4, n, HEAD_BLOCK), BF16),
            jax.ShapeDtypeStruct((4, n // ATT_TILE, HEAD_BLOCK, ATT_TILE), BF16),
        ],
        compiler_params=_params("arbitrary"),
        name="kv_proj",
    )(x2, gain, shift, scale, wn, wvt, kg)


def _compress_kernel(xk_ref, xv_ref, w1_ref, pos_ref, b1_ref, w2_ref, kg_ref, kc_ref, vct_ref):
    half = w1_ref.shape[1] // 2
    rows = xk_ref.shape[2]
    for kv, x_ref in ((0, xk_ref), (1, xv_ref)):
        x = x_ref[0, 0]
        w1 = w1_ref[kv]
        first = _dot(x, w1[:half])
        second = _dot(x, w1[half:])
        hid = first + pltpu.roll(second, rows - 1, axis=0)
        pos_term = _dot(jnp.broadcast_to(pos_ref[kv], (8, 2 * half)).astype(BF16), w1)[0:1]
        hid = jax.nn.gelu(hid + pos_term + b1_ref[kv])
        out = _dot(hid.astype(BF16), w2_ref[kv])
        if kv == 0:
            out = out * lax.rsqrt(jnp.mean(out * out, axis=-1, keepdims=True) + EPS) * kg_ref[...]
            kc_ref[0, 0] = out.astype(BF16)
        else:
            vct_ref[0, 0] = out.T.astype(BF16)


def _compress(xr, w1, pos, b1, w2, kg, batch):
    _, _, rows, width = xr.shape
    g = NSA_KV_GROUPS
    return pl.pallas_call(
        _compress_kernel,
        grid=(batch, g),
        in_specs=[
            pl.BlockSpec((1, 1, rows, width), lambda b, gi: (gi, b, 0, 0)),
            pl.BlockSpec((1, 1, rows, width), lambda b, gi: (g + gi, b, 0, 0)),
            pl.BlockSpec(w1.shape, lambda b, gi: (0, 0, 0)),
            pl.BlockSpec(pos.shape, lambda b, gi: (0, 0, 0)),
            pl.BlockSpec(b1.shape, lambda b, gi: (0, 0, 0)),
            pl.BlockSpec(w2.shape, lambda b, gi: (0, 0, 0)),
            pl.BlockSpec(kg.shape, lambda b, gi: (0, 0)),
        ],
        out_specs=[
            pl.BlockSpec((1, 1, rows, HEAD_BLOCK), lambda b, gi: (b, gi, 0, 0)),
            pl.BlockSpec((1, 1, HEAD_BLOCK, rows), lambda b, gi: (b, gi, 0, 0)),
        ],
        out_shape=[
            jax.ShapeDtypeStruct((batch, g, rows, HEAD_BLOCK), BF16),
            jax.ShapeDtypeStruct((batch, g, HEAD_BLOCK, rows), BF16),
        ],
        compiler_params=_params("arbitrary", "arbitrary"),
        name="compress",
    )(xr, xr, w1, pos, b1, w2, kg)


def _nsa_q_proj_kernel(x_ref, gain_ref, shift_ref, scale_ref, wq_ref, wgt_ref, bg_ref, qg_ref,
                       q_ref, gt_ref):
    h = _norm_mod(x_ref[...], gain_ref[...], shift_ref[0], scale_ref[0]).astype(BF16)
    q = _dot(h, wq_ref[...])
    hb = HEAD_BLOCK
    for c in range(N_HEADS):
        qq = q[:, c * hb:(c + 1) * hb]
        qn = qq * lax.rsqrt(jnp.mean(qq * qq, axis=-1, keepdims=True) + EPS) * qg_ref[...]
        q_ref[c] = qn.astype(BF16)
    gt_ref[...] = jax.nn.sigmoid(_nt_dot(wgt_ref[...], h) + bg_ref[...])


def _nsa_q_proj(x2, gain, shift, scale, wq, wgt, bg, qg, seq):
    n, d = x2.shape
    tm = ROW_TILE
    tiles_per_batch = seq // tm
    const2 = lambda i: (0, 0)
    mod_spec = pl.BlockSpec((1, 1, d), lambda i: (i // tiles_per_batch, 0, 0))
    rows = wgt.shape[0]
    return pl.pallas_call(
        _nsa_q_proj_kernel,
        grid=(n // tm,),
        in_specs=[
            pl.BlockSpec((tm, d), lambda i: (i, 0)),
            pl.BlockSpec((1, d), const2),
            mod_spec, mod_spec,
            pl.BlockSpec(wq.shape, const2),
            pl.BlockSpec(wgt.shape, const2),
            pl.BlockSpec(bg.shape, const2),
            pl.BlockSpec(qg.shape, const2),
        ],
        out_specs=[
            pl.BlockSpec((N_HEADS, tm, HEAD_BLOCK), lambda i: (0, i, 0)),
            pl.BlockSpec((rows, tm), lambda i: (0, i)),
        ],
        out_shape=[
            jax.ShapeDtypeStruct((N_HEADS, n, HEAD_BLOCK), BF16),
            jax.ShapeDtypeStruct((rows, n), F32),
        ],
        compiler_params=_params("arbitrary"),
        name="nsa_q_proj",
    )(x2, gain, shift, scale, wq, wgt, bg, qg)


def _nsa_attn_kernel(n_slc_bias, n_win_bias, top_n,
                     q_ref, g_ref, kc_ref, vct_ref, ks_ref, vst_ref, kw_ref, vwt_ref,
                     cb_ref, sb_ref, wb_ref, ov_ref, o_ref,
                     sel_ref, oc_ref, m_ref, l_ref, acc_ref, os_ref):
    qi = pl.program_id(2)
    nq = pl.num_programs(2)
    t = ATT_TILE
    hg = NSA_HEADS_PER_GROUP
    q = q_ref[...].reshape(hg * t, HEAD_BLOCK)
    n_cmp_rows = kc_ref.shape[2]
    n_blk = ov_ref.shape[0]

    off = pl.multiple_of((nq - 1 - qi) * (t // CMP_STRIDE), t // CMP_STRIDE)
    bias_c = cb_ref[0, pl.ds(off, n_cmp_rows), :]
    sc = _nt_dot(kc_ref[0, 0], q) + bias_c
    visible = bias_c > 0.5 * NEG_INF
    e = jnp.where(visible, jnp.exp2(sc - jnp.max(sc, axis=0, keepdims=True)), 0.0)
    lsum = jnp.sum(e, axis=0, keepdims=True)
    p = e / jnp.where(lsum > 0.0, lsum, 1.0)
    oc_ref[...] = _dot(vct_ref[0, 0], p.astype(BF16))

    psum = p[:, 0:t]
    for i in range(1, hg):
        psum = psum + p[:, i * t:(i + 1) * t]
    p_hi = psum.astype(BF16)
    p_lo = (psum - p_hi.astype(F32)).astype(BF16)
    imp = _dot(ov_ref[...], p_hi) + _dot(ov_ref[...], p_lo)
    blk = lax.broadcasted_iota(jnp.int32, imp.shape, 0)
    pos = qi * t + lax.broadcasted_iota(jnp.int32, imp.shape, 1)
    back = pos // SEL_BLOCK - blk
    forced = (blk == 0) | ((back >= 0) & (back < SEL_N_LOCAL))
    imp = jnp.where(forced, FORCE, imp)
    imp = jnp.where(back >= 0, imp, NEG_INF)
    sel = jnp.zeros(imp.shape, F32)
    for _ in range(top_n):
        m = jnp.max(imp, axis=0, keepdims=True)
        idx = jnp.min(jnp.where(imp == m, blk, n_blk), axis=0, keepdims=True)
        hit = blk == idx
        sel = jnp.where(hit & (m >= 0.0), 1.0, sel)
        imp = jnp.where(hit, -jnp.inf, imp)
    sel_ref[...] = jnp.where(sel > 0.0, 0.0, NEG_INF)

    blocks_per_tile = t // SEL_BLOCK

    _softmax_init(m_ref, l_ref, acc_ref)

    def slc_body(kt, carry):
        k = ks_ref[0, pl.ds(pl.multiple_of(kt * t, t), t), :]
        bias = sb_ref[0, jnp.minimum(qi - kt, n_slc_bias - 1)]
        rows = [jnp.broadcast_to(sel_ref[pl.ds(kt * blocks_per_tile + jj, 1), :], (SEL_BLOCK, t))
                for jj in range(blocks_per_tile)]
        mask = jnp.concatenate(rows, axis=0)
        s = _nt_dot(k, q) + bias + jnp.concatenate([mask] * hg, axis=1)
        _online_softmax_step(s, vst_ref[0, kt], m_ref, l_ref, acc_ref)
        return carry

    lax.fori_loop(0, qi + 1, slc_body, 0)
    os_ref[...] = acc_ref[...] / l_ref[...]

    _softmax_init(m_ref, l_ref, acc_ref)

    def win_body(kt, carry):
        k = kw_ref[0, pl.ds(pl.multiple_of(kt * t, t), t), :]
        s = _nt_dot(k, q) + wb_ref[0, qi - kt]
        _online_softmax_step(s, vwt_ref[0, kt], m_ref, l_ref, acc_ref)
        return carry

    lax.fori_loop(jnp.maximum(qi - (n_win_bias - 1), 0), qi + 1, win_body, 0)
    ow = acc_ref[...] / l_ref[...]

    gates = g_ref[...]
    for i in range(hg):
        sl = slice(i * t, (i + 1) * t)
        o = (gates[3 * i:3 * i + 1] * oc_ref[:, sl] + gates[3 * i + 1:3 * i + 2] * os_ref[:, sl]
             + gates[3 * i + 2:3 * i + 3] * ow[:, sl])
        o_ref[:, i * HEAD_BLOCK:(i + 1) * HEAD_BLOCK] = o.T.astype(BF16)


def _nsa_attn(q, gt, kc, vct, k, vt, cmp_bias, slc_bias, win_bias, ovt, batch, seq):
    t = ATT_TILE
    nq = seq // t
    hg = NSA_HEADS_PER_GROUP
    g = NSA_KV_GROUPS
    n = batch * seq
    top_n = min(SEL_TOPN, seq // SEL_BLOCK)
    lanes = hg * t
    idx3 = lambda b, gi, i: (gi, 0, 0)
    idx4 = lambda b, gi, i: (gi, 0, 0, 0)
    return pl.pallas_call(
        functools.partial(_nsa_attn_kernel, slc_bias.shape[1], win_bias.shape[1], top_n),
        grid=(batch, g, nq),
        in_specs=[
            pl.BlockSpec((hg, t, HEAD_BLOCK), lambda b, gi, i: (gi, b * nq + i, 0)),
            pl.BlockSpec((GATE_ROWS, t), lambda b, gi, i: (gi, b * nq + i)),
            pl.BlockSpec((1, 1) + kc.shape[2:], lambda b, gi, i: (b, gi, 0, 0)),
            pl.BlockSpec((1, 1) + vct.shape[2:], lambda b, gi, i: (b, gi, 0, 0)),
            pl.BlockSpec((1, seq, HEAD_BLOCK), lambda b, gi, i: (gi, b, 0)),
            pl.BlockSpec((1, nq, HEAD_BLOCK, t), lambda b, gi, i: (gi, b, 0, 0)),
            pl.BlockSpec((1, seq, HEAD_BLOCK), lambda b, gi, i: (g + gi, b, 0)),
            pl.BlockSpec((1, nq, HEAD_BLOCK, t), lambda b, gi, i: (g + gi, b, 0, 0)),
            pl.BlockSpec((1,) + cmp_bias.shape[1:], idx3),
            pl.BlockSpec((1,) + slc_bias.shape[1:], idx4),
            pl.BlockSpec((1,) + win_bias.shape[1:], idx4),
            pl.BlockSpec(ovt.shape, lambda b, gi, i: (0, 0)),
        ],
        out_specs=pl.BlockSpec((t, hg * HEAD_BLOCK), lambda b, gi, i: (b * nq + i, gi)),
        out_shape=jax.ShapeDtypeStruct((n, N_HEADS * HEAD_BLOCK), BF16),
        scratch_shapes=[
            pltpu.VMEM((ovt.shape[0], t), F32),
            pltpu.VMEM((HEAD_BLOCK, lanes), F32),
            pltpu.VMEM((1, lanes), F32),
            pltpu.VMEM((1, lanes), F32),
            pltpu.VMEM((HEAD_BLOCK, lanes), F32),
            pltpu.VMEM((HEAD_BLOCK, lanes), F32),
        ],
        compiler_params=_params("arbitrary", "arbitrary", "arbitrary"),
        name="nsa_attn",
    )(q, gt, kc, vct, k, vt, k, vt, cmp_bias, slc_bias, win_bias, ovt)


def _toeplitz_kernel(t, n_tiles, v_ref, o_ref):
    sub = 8
    x = jnp.broadcast_to(v_ref[0], (sub, v_ref.shape[2]))
    for g in range(t // sub):
        shifted = pltpu.roll(x, sub * g, 1, stride=1, stride_axis=0)
        for d in range(n_tiles):
            o_ref[0, d, sub * g:sub * (g + 1), :] = shifted[:, (d + 1) * t:(d + 2) * t]


def _bias_tiles(profile, t, n_tiles, window=None):
    heads = profile.shape[0]
    span = n_tiles * t
    vis = profile[:, :span]
    if window is not None:
        vis = jnp.where(jnp.arange(span)[None] < window, vis, NEG_INF)
    v = jnp.concatenate([jnp.full((heads, t), NEG_INF, F32), vis], axis=1)
    length = span + t
    return pl.pallas_call(
        functools.partial(_toeplitz_kernel, t, n_tiles),
        grid=(heads,),
        in_specs=[pl.BlockSpec((1, 1, length), lambda h: (h, 0, 0))],
        out_specs=pl.BlockSpec((1, n_tiles, t, t), lambda h: (h, 0, 0, 0)),
        out_shape=jax.ShapeDtypeStruct((heads, n_tiles, t, t), F32),
        compiler_params=_params("arbitrary"),
        name="bias_tiles",
    )(v.reshape(heads, 1, length))


def _cmp_bias_strip(profile, seq):
    t = ATT_TILE
    heads = profile.shape[0]
    per = t // CMP_STRIDE
    n_rho = 2 * (seq // t) - 1
    c = seq - t - (CMP_LEN - 1)
    pad = t * (n_rho - 1) + CMP_STRIDE * (per - 1) - c
    g = jnp.concatenate([jnp.full((heads, pad), NEG_INF, F32), profile], axis=1)
    slabs = []
    for sigma in range(per):
        base = c - CMP_STRIDE * sigma + pad
        seg = g[:, base - t * (n_rho - 1):base + t].reshape(heads, n_rho, t)
        slabs.append(seg[:, ::-1])
    return jnp.stack(slabs, axis=2).reshape(heads, n_rho * per, t)


def _n_far_tiles(t):
    return -(-(REL_MAX_DIST + t - 1) // t) + 1


def _group_lanes(tiles):
    hg = NSA_HEADS_PER_GROUP
    parts = tiles.reshape((NSA_KV_GROUPS, hg) + tiles.shape[1:])
    return jnp.concatenate([parts[:, i] for i in range(hg)], axis=-1)


def kernel(x, c, rel_bias, ada_w, ada_b, norm_mix, norm_ffn, diff_w_in, diff_w_out, diff_lambda,
           diff_q_gain, diff_k_gain, diff_subln, kv_norm, kv_ada_w, kv_ada_b, kv_w, cmp_pos, cmp_w1,
           cmp_b1, cmp_w2, nsa_k_gain, nsa_w_in, nsa_b_gate, nsa_w_out, nsa_q_gain, peer_w_q,
           peer_subkeys, peer_u, peer_v):
    batch, seq, d = x.shape
    n = batch * seq
    t = ATT_TILE
    nq = seq // t
    width = N_HEADS * HEAD_BLOCK
    x2 = x.reshape(n, d)

    c_pad = jnp.pad(c, ((0, 8 - batch % 8 if batch % 8 else 0), (0, 0)))
    mods = _ada(c_pad, ada_w, ada_b, 1536)
    kv_mods = _ada(c_pad, kv_ada_w[None], kv_ada_b[None], 1024)

    def mod(arr, layer, k):
        return arr[layer, :batch, k * d:(k + 1) * d].reshape(batch, 1, d)

    profile = rel_bias[_rel_bucket(jnp.arange(seq))].T * LOG2_E

    lam_init = 0.8 - 0.6 * math.exp(-0.3 * 0)
    w_in = diff_w_in[0]
    wqk = w_in[:, :2 * width].astype(BF16)
    wvt = w_in[:, 2 * width:].T.astype(BF16)
    grp = jnp.asarray(np.kron(np.eye(width // DIFF_HEAD_DIM), np.ones((DIFF_HEAD_DIM, DIFF_HEAD_DIM))), BF16)
    reps = width // DIFF_HEAD_DIM
    qg = (jnp.tile(diff_q_gain[0], reps) * (DIFF_HEAD_DIM ** -0.5 * LOG2_E)).reshape(1, width)
    kg = jnp.tile(diff_k_gain[0], reps).reshape(1, width)
    q, k, vt = _diff_proj(x2, norm_mix[0:1], mod(mods, 0, 0), mod(mods, 0, 1), wqk, wvt, grp, qg, kg, seq)
    diff_bias = _bias_tiles(profile, DIFF_TILE, min(_n_far_tiles(DIFF_TILE), seq // DIFF_TILE))
    sub_col = (diff_subln[0] * (1.0 - lam_init)).reshape(HEAD_BLOCK, 1)
    o = _diff_attn(q, k, vt, diff_bias, diff_lambda[0], sub_col, batch, seq, lam_init)
    x2 = _out_proj(x2, o, diff_w_out[0].astype(BF16), mod(mods, 0, 2), seq)

    def peer(x2, layer):
        return _peer_layer(x2, norm_ffn[layer:layer + 1], mod(mods, layer, 3), mod(mods, layer, 4),
                           mod(mods, layer, 5), peer_w_q[layer].T.astype(BF16),
                           peer_subkeys[layer].astype(BF16), peer_u[layer].astype(BF16),
                           peer_v[layer].astype(BF16), seq)

    x2 = peer(x2, 0)

    hb = HEAD_BLOCK

    def kv_cols(branch, kv, g):
        start = branch * 4 * hb + kv * 2 * hb + g * hb
        return kv_w[:, start:start + hb]

    wn = jnp.concatenate([kv_cols(0, 0, 0), kv_cols(0, 0, 1), kv_cols(0, 1, 0), kv_cols(0, 1, 1),
                          kv_cols(1, 0, 0), kv_cols(1, 0, 1), kv_cols(2, 0, 0), kv_cols(2, 0, 1)],
                         axis=1).astype(BF16)
    wvt_kv = jnp.concatenate([kv_cols(1, 1, 0), kv_cols(1, 1, 1), kv_cols(2, 1, 0), kv_cols(2, 1, 1)],
                             axis=1).T.astype(BF16)
    cmp_raw, k_sw, vt_sw = _kv_proj(x2, kv_norm.reshape(1, d), mod(kv_mods, 0, 0), mod(kv_mods, 0, 1),
                                    wn, wvt_kv, nsa_k_gain[1:3], seq)
    xr = cmp_raw.reshape(4, batch, seq // CMP_STRIDE, CMP_STRIDE * hb)
    kc, vct = _compress(xr, cmp_w1.astype(BF16), cmp_pos.reshape(2, 1, CMP_LEN * hb),
                        cmp_b1.reshape(2, 1, CMP_HIDDEN), cmp_w2.astype(BF16), nsa_k_gain[0:1], batch)

    hg = NSA_HEADS_PER_GROUP
    w_in1 = nsa_w_in[0]
    wq = w_in1[:, :width].astype(BF16)
    w_gate = w_in1[:, width:].reshape(d, NSA_KV_GROUPS, hg * N_BRANCH)
    w_gate = jnp.pad(w_gate, ((0, 0), (0, 0), (0, GATE_ROWS - hg * N_BRANCH)))
    wgt = w_gate.reshape(d, NSA_KV_GROUPS * GATE_ROWS).T.astype(BF16)
    b_gate = jnp.pad(nsa_b_gate[0].reshape(NSA_KV_GROUPS, hg * N_BRANCH),
                     ((0, 0), (0, GATE_ROWS - hg * N_BRANCH))).reshape(NSA_KV_GROUPS * GATE_ROWS, 1)
    qg1 = (nsa_q_gain[0] * (hb ** -0.5 * LOG2_E)).reshape(1, hb)
    q1, gt = _nsa_q_proj(x2, norm_mix[1:2], mod(mods, 1, 0), mod(mods, 1, 1), wq, wgt, b_gate, qg1, seq)

    slc_bias = _group_lanes(_bias_tiles(profile, t, min(_n_far_tiles(t), nq)))
    win_bias = _group_lanes(_bias_tiles(profile, t, min(WINDOW // t + 1, nq), window=WINDOW))
    cmp_bias = _group_lanes(_cmp_bias_strip(profile, seq))
    n_rows = seq // CMP_STRIDE
    n_sel = seq // SEL_BLOCK
    cmp_start = np.arange(n_rows) * CMP_STRIDE
    sel_start = np.arange(n_sel) * SEL_BLOCK
    overlap = np.clip(np.minimum(cmp_start[:, None] + CMP_LEN, sel_start[None, :] + SEL_BLOCK)
                      - np.maximum(cmp_start[:, None], sel_start[None, :]), 0, None) / CMP_LEN
    overlap[(seq - CMP_LEN) // CMP_STRIDE + 1:] = 0.0
    ovt = jnp.asarray(overlap.T, BF16)
    o1 = _nsa_attn(q1, gt, kc, vct, k_sw, vt_sw, cmp_bias, slc_bias, win_bias, ovt, batch, seq)
    x2 = _out_proj(x2, o1, nsa_w_out[0].astype(BF16), mod(mods, 1, 2), seq)

    x2 = peer(x2, 1)
    return x2.reshape(batch, seq, d)
```
